```python
import math
import jax
import jax.numpy as jnp
from jax import lax
import numpy as np

D_MODEL = 1024
BATCH = 8
SEQ = 4096
DEPTH = 2

D_MIX = D_MODEL
D_CONV = D_MIX // 2
D_ATTN = D_MIX - D_CONV
HEAD_DIM = 64
N_HEADS = D_ATTN // HEAD_DIM
CONV_K = 3
IN_COLS = 3 * D_CONV + 3 * D_ATTN
QBLK = 128
N_EXPERTS = 32
N_GROUPS = 8
EXPERTS_PER_GROUP = N_EXPERTS // N_GROUPS
TOP_K = 2
D_EXPERT = 512
MOE_BLK = 128
DEEPNORM_ALPHA = (2.0 * DEPTH) ** 0.25
DEEPNORM_BETA = (8.0 * DEPTH) ** -0.25
LN_EPS = 1e-5
RMS_EPS = 1e-6

kernel_name = "hybrid_conv_stickbreaking_sharedrouter_moe_deepnorm"


def layer_norm(x, g, b):
    xf = x.astype(jnp.float32)
    mu = jnp.mean(xf, axis=-1, keepdims=True)
    var = jnp.mean(jnp.square(xf - mu), axis=-1, keepdims=True)
    y = (xf - mu) * lax.rsqrt(var + LN_EPS) * g.astype(jnp.float32) + b.astype(jnp.float32)
    return y.astype(x.dtype)


def rms_norm(x, g):
    xf = x.astype(jnp.float32)
    y = xf * lax.rsqrt(jnp.mean(jnp.square(xf), axis=-1, keepdims=True) + RMS_EPS)
    return (y * g.astype(jnp.float32)).astype(x.dtype)


def causal_short_conv(u, w):
    c = u.shape[-1]
    return lax.conv_general_dilated(
        u, w[:, None, :].astype(u.dtype), window_strides=(1,),
        padding=((CONV_K - 1, 0),), dimension_numbers=("NWC", "WIO", "NWC"),
        feature_group_count=c)


def stick_breaking_attention(q, k, v):
    bsz, s, _ = q.shape
    nb = s // QBLK
    scale = 1.0 / math.sqrt(HEAD_DIM)
    qh = q.reshape(bsz, s, N_HEADS, HEAD_DIM).transpose(0, 2, 1, 3)
    kf = k.reshape(bsz, s, N_HEADS, HEAD_DIM).transpose(0, 2, 1, 3).astype(jnp.float32)
    vf = v.reshape(bsz, s, N_HEADS, HEAD_DIM).transpose(0, 2, 1, 3).astype(jnp.float32)
    qb = qh.reshape(bsz, N_HEADS, nb, QBLK, HEAD_DIM).transpose(2, 0, 1, 3, 4)
    kpos = jnp.arange(s)

    def block(args):
        qblk, i = args
        z = jnp.einsum("bhqd,bhkd->bhqk", qblk.astype(jnp.float32), kf) * scale
        qpos = i * QBLK + jnp.arange(QBLK)
        mask = kpos[None, :] < qpos[:, None]
        log_keep = jnp.where(mask, jax.nn.log_sigmoid(-z), 0.0)
        suffix = lax.cumsum(log_keep, axis=3, reverse=True) - log_keep
        a = jnp.where(mask, jnp.exp(jax.nn.log_sigmoid(z) + suffix), 0.0)
        return jnp.einsum("bhqk,bhkd->bhqd", a, vf)

    o = lax.map(block, (qb, jnp.arange(nb)))
    o = o.transpose(1, 0, 3, 2, 4).reshape(bsz, s, D_ATTN)
    return o.astype(q.dtype)


def route(h, router_w, router_b):
    t = h.shape[0]
    scores = jax.nn.sigmoid(jnp.matmul(h, router_w).astype(jnp.float32))
    sel = (scores + router_b.astype(jnp.float32)).reshape(t, N_GROUPS, EXPERTS_PER_GROUP)
    group_score = lax.top_k(sel, 2)[0].sum(-1)
    g = jnp.argmax(group_score, axis=-1)
    in_group = jnp.take_along_axis(sel, g[:, None, None], axis=1)[:, 0]
    _, local = lax.top_k(in_group, TOP_K)
    eidx = g[:, None] * EXPERTS_PER_GROUP + local
    w = jnp.take_along_axis(scores, eidx, axis=1)
    w = w / jnp.sum(w, axis=-1, keepdims=True)
    return eidx, w


def moe(h, router_w, router_b, w_gate, w_up, w_down):
    t, d = h.shape
    eidx, w = route(h, router_w, router_b)
    n = t * TOP_K
    flat_e = eidx.reshape(n)
    flat_t = jnp.repeat(jnp.arange(t, dtype=jnp.int32), TOP_K, total_repeat_length=n)
    flat_w = w.reshape(n)
    order = jnp.argsort(flat_e)
    se, st, sw = flat_e[order], flat_t[order], flat_w[order]
    counts = jnp.zeros((N_EXPERTS,), jnp.int32).at[flat_e].add(1)
    starts = jnp.cumsum(counts) - counts
    padded = (counts + MOE_BLK - 1) // MOE_BLK * MOE_BLK
    pends = jnp.cumsum(padded)
    pstarts = pends - padded
    dest = pstarts[se] + jnp.arange(n, dtype=jnp.int32) - starts[se]
    rows = (n + N_EXPERTS * (MOE_BLK - 1) + MOE_BLK - 1) // MOE_BLK * MOE_BLK
    nblk = rows // MOE_BLK
    buf_t = jnp.full((rows,), t, jnp.int32).at[dest].set(st)
    buf_w = jnp.zeros((rows,), jnp.float32).at[dest].set(sw)
    blk_e = jnp.clip(jnp.searchsorted(pends, jnp.arange(nblk) * MOE_BLK, side="right"),
                     0, N_EXPERTS - 1)
    hp = jnp.concatenate([h, jnp.zeros((1, d), h.dtype)], axis=0)
    xb = hp[buf_t].reshape(nblk, MOE_BLK, d)

    def expert_block(args):
        xblk, e = args
        a = jnp.matmul(xblk, w_gate[e])
        u = jnp.matmul(xblk, w_up[e])
        return jnp.matmul(jax.nn.silu(a) * u, w_down[e])

    yb = lax.map(expert_block, (xb, blk_e)).reshape(rows, d)
    out = jnp.zeros((t + 1, d), h.dtype).at[buf_t].add(yb * buf_w[:, None].astype(h.dtype))
    return out[:t]


def setup_inputs(seed: int = 0) -> dict:
    key = jax.random.key(seed)
    ks = jax.random.split(key, 18)
    nrm = jax.random.normal
    f32 = jnp.float32
    return {
        "x": nrm(ks[0], (BATCH, SEQ, D_MODEL), f32),
        "ln0_g": 1.0 + 0.02 * nrm(ks[1], (D_MODEL,), f32),
        "ln0_b": 0.02 * nrm(ks[2], (D_MODEL,), f32),
        "w_in": nrm(ks[3], (DEPTH, D_MODEL, IN_COLS), f32) * D_MODEL ** -0.5,
        "conv_w": nrm(ks[4], (DEPTH, CONV_K, D_CONV), f32) * CONV_K ** -0.5,
        "g_conv": 1.0 + 0.02 * nrm(ks[5], (DEPTH, D_CONV), f32),
        "g_attn": 1.0 + 0.02 * nrm(ks[6], (DEPTH, D_ATTN), f32),
        "w_out": nrm(ks[7], (DEPTH, D_MIX, D_MODEL), f32) * (D_MIX ** -0.5 * DEEPNORM_BETA),
        "ln_mix_g": 1.0 + 0.02 * nrm(ks[8], (DEPTH, D_MODEL), f32),
        "ln_mix_b": 0.02 * nrm(ks[9], (DEPTH, D_MODEL), f32),
        "router_w": nrm(ks[10], (D_MODEL, N_EXPERTS), f32) * D_MODEL ** -0.5,
        "router_b": 0.01 * nrm(ks[11], (N_EXPERTS,), f32),
        "w_gate": nrm(ks[12], (DEPTH, N_EXPERTS, D_MODEL, D_EXPERT), f32) * D_MODEL ** -0.5,
        "w_up": nrm(ks[13], (DEPTH, N_EXPERTS, D_MODEL, D_EXPERT), f32) * D_MODEL ** -0.5,
        "w_down": nrm(ks[14], (DEPTH, N_EXPERTS, D_EXPERT, D_MODEL), f32) * (D_EXPERT ** -0.5 * DEEPNORM_BETA),
        "ln_ffn_g": 1.0 + 0.02 * nrm(ks[15], (DEPTH, D_MODEL), f32),
        "ln_ffn_b": 0.02 * nrm(ks[16], (DEPTH, D_MODEL), f32),
    }


def reference(x, ln0_g, ln0_b, w_in, conv_w, g_conv, g_attn, w_out, ln_mix_g, ln_mix_b,
              router_w, router_b, w_gate, w_up, w_down, ln_ffn_g, ln_ffn_b):
    bsz, s, d = x.shape
    h = layer_norm(x, ln0_g, ln0_b)
    for l in range(DEPTH):
        proj = jnp.matmul(h, w_in[l])
        cb, cc, cx, q, k, v = jnp.split(
            proj, [D_CONV, 2 * D_CONV, 3 * D_CONV, 3 * D_CONV + D_ATTN,
                   3 * D_CONV + 2 * D_ATTN], axis=-1)
        conv_out = cb * causal_short_conv(cc * cx, conv_w[l])
        attn_out = stick_breaking_attention(q, k, v)
        mixed = jnp.concatenate([rms_norm(conv_out, g_conv[l]),
                                 rms_norm(attn_out, g_attn[l])], axis=-1)
        mix = jnp.matmul(mixed, w_out[l])
        h = layer_norm(DEEPNORM_ALPHA * h + mix, ln_mix_g[l], ln_mix_b[l])
        ffn = moe(h.reshape(bsz * s, d), router_w, router_b,
                  w_gate[l], w_up[l], w_down[l]).reshape(bsz, s, d)
        h = layer_norm(DEEPNORM_ALPHA * h + ffn, ln_ffn_g[l], ln_ffn_b[l])
    return h
```

```python
import functools
import math

import jax
import jax.numpy as jnp
from jax import lax
from jax.experimental import pallas as pl
from jax.experimental.pallas import tpu as pltpu

D_MODEL = 1024
D_CONV = 512
D_ATTN = 512
HEAD_DIM = 64
N_HEADS = D_ATTN // HEAD_DIM
CONV_K = 3
N_EXPERTS = 32
N_GROUPS = 8
EXPERTS_PER_GROUP = 4
D_EXPERT = 512
LN_EPS = 1e-5
RMS_EPS = 1e-6

PAIR_A = (0, 0, 0, 1, 1, 2)
PAIR_B = (1, 2, 3, 3, 2, 3)
N_PAIRS = len(PAIR_A)
N_CLASSES = N_GROUPS * N_PAIRS

LANES = 128
SUBLANES = 8
EXT_COLS = D_MODEL + LANES

PROJ_ROWS = 512
ATTN_ROWS = 256
MIX_ROWS = 256
RANK_ROWS = 512
MOE_ROWS = 128
PERM_ROWS = 2048

ATTN_STOP = -88.0

VMEM_LIMIT = 56 * 1024 * 1024

_BF16 = jnp.bfloat16
_F32 = jnp.float32


def _dot(a, b):
    return jnp.dot(a, b, preferred_element_type=_F32)


def _dot_nt(a, b):
    return lax.dot_general(a, b, (((1,), (1,)), ((), ())), preferred_element_type=_F32)


def _layer_norm(x, g, b):
    mu = jnp.mean(x, axis=-1, keepdims=True)
    xc = x - mu
    var = jnp.mean(xc * xc, axis=-1, keepdims=True)
    return xc * lax.rsqrt(var + LN_EPS) * g + b


def _rms_norm(x, g):
    ms = jnp.mean(x * x, axis=-1, keepdims=True)
    return x * lax.rsqrt(ms + RMS_EPS) * g


def _proj_kernel(apply_ln, tiles_per_seq, *refs):
    if apply_ln:
        x_ref, g0_ref, b0_ref, w_ref, cw_ref, gc_ref, h_ref, mc_ref, qkv_ref, carry_ref = refs
    else:
        x_ref, w_ref, cw_ref, gc_ref, mc_ref, qkv_ref, carry_ref = refs
    i = pl.program_id(0)
    rows = x_ref.shape[0]

    h = x_ref[...]
    if apply_ln:
        h = _layer_norm(h, g0_ref[...], b0_ref[...])
        h_ref[...] = h
    hb = h.astype(_BF16)

    cb = _dot(hb, w_ref[:, 0:D_CONV])
    cc = _dot(hb, w_ref[:, D_CONV:2 * D_CONV])
    cx = _dot(hb, w_ref[:, 2 * D_CONV:3 * D_CONV])
    u = cc * cx

    @pl.when(i % tiles_per_seq == 0)
    def _():
        carry_ref[...] = jnp.zeros_like(carry_ref)

    prev = carry_ref[...]
    p1 = prev[SUBLANES - 1:SUBLANES, :]
    p2 = prev[SUBLANES - 2:SUBLANES - 1, :]
    row = lax.broadcasted_iota(jnp.int32, (rows, 1), 0)
    u1 = jnp.where(row == 0, p1, pltpu.roll(u, 1, 0))
    u2 = jnp.where(row == 0, p2, jnp.where(row == 1, p1, pltpu.roll(u, 2, 0)))
    carry_ref[...] = u[rows - SUBLANES:rows, :]

    cw = cw_ref[...]
    conv = cb * (cw[0:1, :] * u2 + cw[1:2, :] * u1 + cw[2:3, :] * u)
    mc_ref[...] = _rms_norm(conv, gc_ref[...]).astype(_BF16)

    qkv = _dot(hb, w_ref[:, 3 * D_CONV:])
    qkv_ref[:, 0:D_ATTN] = (qkv[:, 0:D_ATTN] * (1.0 / math.sqrt(HEAD_DIM))).astype(_BF16)
    qkv_ref[:, D_ATTN:] = qkv[:, D_ATTN:].astype(_BF16)


def _proj_conv(x2d, ln0, w_in_bf, conv_w, g_conv, seq):
    t = x2d.shape[0]
    rows = PROJ_ROWS
    apply_ln = ln0 is not None
    in_cols = w_in_bf.shape[1]
    row_spec = lambda c: pl.BlockSpec((rows, c), lambda i: (i, 0))
    full = lambda a: pl.BlockSpec(a.shape, lambda i: (0,) * a.ndim)
    args, in_specs = [x2d], [row_spec(D_MODEL)]
    if apply_ln:
        args += [ln0[0], ln0[1]]
        in_specs += [full(ln0[0]), full(ln0[1])]
    args += [w_in_bf, conv_w, g_conv]
    in_specs += [full(w_in_bf), full(conv_w), full(g_conv)]
    out_shape, out_specs = [], []
    if apply_ln:
        out_shape.append(jax.ShapeDtypeStruct((t, D_MODEL), _F32))
        out_specs.append(row_spec(D_MODEL))
    out_shape += [jax.ShapeDtypeStruct((t, D_CONV), _BF16),
                  jax.ShapeDtypeStruct((t, in_cols - 3 * D_CONV), _BF16)]
    out_specs += [row_spec(D_CONV), row_spec(in_cols - 3 * D_CONV)]
    return pl.pallas_call(
        functools.partial(_proj_kernel, apply_ln, seq // rows),
        out_shape=out_shape,
        grid=(t // rows,),
        in_specs=in_specs,
        out_specs=out_specs,
        scratch_shapes=[pltpu.VMEM((SUBLANES, D_CONV), _F32)],
        compiler_params=pltpu.CompilerParams(
            dimension_semantics=("arbitrary",), vmem_limit_bytes=VMEM_LIMIT),
        name="proj_conv",
    )(*args)


def _attn_kernel(q_ref, k_ref, v_ref, ux_ref, ga_ref, o_ref, acc_ref, r_ref):
    i = pl.program_id(1)
    tq = q_ref.shape[0]
    tk = tq

    qpos = lax.broadcasted_iota(jnp.int32, (tq, tk), 0)
    kpos = lax.broadcasted_iota(jnp.int32, (tq, tk), 1)
    causal = kpos < qpos

    def block(j, diagonal):
        ks = pl.multiple_of(j * tk, tk)
        for hd in range(N_HEADS):
            cols = slice(hd * HEAD_DIM, (hd + 1) * HEAD_DIM)
            z = _dot_nt(q_ref[:, cols], k_ref[pl.ds(ks, tk), cols])
            sp = jnp.maximum(z, 0.0) + jnp.log(1.0 + jnp.exp(-jnp.abs(z)))
            log_keep = jnp.where(causal, -sp, 0.0) if diagonal else -sp
            sx = _dot(log_keep.astype(_BF16), ux_ref[...])
            if diagonal:
                arg = z - sp + sx[:, 0:tk]
                a = jnp.where(causal, jnp.exp(arg), 0.0)
                acc_ref[:, cols] = _dot(a.astype(_BF16), v_ref[pl.ds(ks, tk), cols])
                r_ref[hd] = sx[:, tk:]
            else:
                r = r_ref[hd]
                arg = z - sp + sx[:, 0:tk] + jnp.concatenate([r] * (tk // LANES), axis=1)
                a = jnp.exp(arg)
                acc_ref[:, cols] += _dot(a.astype(_BF16), v_ref[pl.ds(ks, tk), cols])
                r_ref[hd] = r + sx[:, tk:]

    block(i, True)

    def cond(j):
        return jnp.logical_and(j >= 0, jnp.max(r_ref[...]) > ATTN_STOP)

    def body(j):
        block(j, False)
        return j - 1

    lax.while_loop(cond, body, i - 1)

    o_ref[...] = _rms_norm(acc_ref[...], ga_ref[...]).astype(_BF16)


def _attention(qkv3, ux, g_attn):
    b, s, _ = qkv3.shape
    tq = ATTN_ROWS
    return pl.pallas_call(
        _attn_kernel,
        out_shape=jax.ShapeDtypeStruct((b, s, D_ATTN), _BF16),
        grid=(b, s // tq),
        in_specs=[
            pl.BlockSpec((None, tq, D_ATTN), lambda bi, i: (bi, i, 0)),
            pl.BlockSpec((None, s, D_ATTN), lambda bi, i: (bi, 0, 1)),
            pl.BlockSpec((None, s, D_ATTN), lambda bi, i: (bi, 0, 2)),
            pl.BlockSpec(ux.shape, lambda bi, i: (0, 0)),
            pl.BlockSpec(g_attn.shape, lambda bi, i: (0, 0)),
        ],
        out_specs=pl.BlockSpec((None, tq, D_ATTN), lambda bi, i: (bi, i, 0)),
        scratch_shapes=[pltpu.VMEM((tq, D_ATTN), _F32),
                        pltpu.VMEM((N_HEADS, tq, LANES), _F32)],
        compiler_params=pltpu.CompilerParams(
            dimension_semantics=("arbitrary", "arbitrary"), vmem_limit_bytes=VMEM_LIMIT),
        name="attention",
    )(qkv3, qkv3, qkv3, ux, g_attn)


def _mix_kernel(alpha, mc_ref, ma_ref, h_ref, wo_ref, g_ref, b_ref, rwh_ref, rwl_ref, rb_ref,
                hx_ref, cls_ref):
    rows = h_ref.shape[0]
    mix = _dot(mc_ref[...], wo_ref[0:D_CONV, :]) + _dot(ma_ref[...], wo_ref[D_CONV:, :])
    h1 = _layer_norm(alpha * h_ref[...] + mix, g_ref[...], b_ref[...])
    hx_ref[:, 0:D_MODEL] = h1

    hi = h1.astype(_BF16)
    lo = (h1 - hi.astype(_F32)).astype(_BF16)
    logits = (_dot_nt(rwh_ref[...], hi) + _dot_nt(rwh_ref[...], lo) + _dot_nt(rwl_ref[...], hi))
    score = 1.0 / (1.0 + jnp.exp(-logits))
    sel = score + rb_ref[...]
    s = [sel[j * N_GROUPS:(j + 1) * N_GROUPS, :] for j in range(EXPERTS_PER_GROUP)]
    sc = [score[j * N_GROUPS:(j + 1) * N_GROUPS, :] for j in range(EXPERTS_PER_GROUP)]

    gs = None
    for a in range(EXPERTS_PER_GROUP):
        for c in range(a + 1, EXPERTS_PER_GROUP):
            pair = s[a] + s[c]
            gs = pair if gs is None else jnp.maximum(gs, pair)
    gidx = lax.broadcasted_iota(jnp.int32, gs.shape, 0)
    gmax = jnp.max(gs, axis=0, keepdims=True)
    g = jnp.min(jnp.where(gs == gmax, gidx, N_GROUPS), axis=0, keepdims=True)
    in_g = gidx == g
    sv = [jnp.sum(jnp.where(in_g, s[j], 0.0), axis=0, keepdims=True) for j in range(EXPERTS_PER_GROUP)]
    cv = [jnp.sum(jnp.where(in_g, sc[j], 0.0), axis=0, keepdims=True) for j in range(EXPERTS_PER_GROUP)]

    chosen = []
    for j in range(EXPERTS_PER_GROUP):
        rank = jnp.zeros(sv[j].shape, jnp.int32)
        for c in range(EXPERTS_PER_GROUP):
            if c == j:
                continue
            ahead = (sv[c] >= sv[j]) if c < j else (sv[c] > sv[j])
            rank = rank + ahead.astype(jnp.int32)
        chosen.append(rank < 2)
    den = sum(jnp.where(chosen[j], cv[j], 0.0) for j in range(EXPERTS_PER_GROUP))
    wgt = [jnp.where(chosen[j], cv[j] / den, 0.0) for j in range(EXPERTS_PER_GROUP)]
    first = jnp.full(den.shape, EXPERTS_PER_GROUP, jnp.int32)
    last = jnp.full(den.shape, -1, jnp.int32)
    for j in range(EXPERTS_PER_GROUP):
        first = jnp.where(chosen[j], jnp.minimum(first, j), first)
        last = jnp.where(chosen[j], jnp.maximum(last, j), last)
    pair_id = jnp.zeros(den.shape, jnp.int32)
    for p in range(N_PAIRS):
        pair_id = jnp.where(jnp.logical_and(first == PAIR_A[p], last == PAIR_B[p]), p, pair_id)
    w_first = sum(jnp.where(first == j, wgt[j], 0.0) for j in range(EXPERTS_PER_GROUP))
    w_last = sum(jnp.where(last == j, wgt[j], 0.0) for j in range(EXPERTS_PER_GROUP))
    cls_ref[...] = g * N_PAIRS + pair_id

    sub = lax.broadcasted_iota(jnp.int32, (LANES, rows), 0)
    slab = jnp.where(sub == 0, w_first, jnp.where(sub == 1, w_last, 0.0))
    hx_ref[:, D_MODEL:] = slab.T


def _mix_route(alpha, mc, ma, h, w_out_bf, ln_g, ln_b, rw_hi, rw_lo, rb):
    t = h.shape[0]
    rows = MIX_ROWS
    row_spec = lambda c: pl.BlockSpec((rows, c), lambda i: (i, 0))
    full = lambda a: pl.BlockSpec(a.shape, lambda i: (0,) * a.ndim)
    return pl.pallas_call(
        functools.partial(_mix_kernel, alpha),
        out_shape=[jax.ShapeDtypeStruct((t, EXT_COLS), _F32),
                   jax.ShapeDtypeStruct((1, t), jnp.int32)],
        grid=(t // rows,),
        in_specs=[row_spec(D_CONV), row_spec(D_ATTN), row_spec(D_MODEL), full(w_out_bf),
                  full(ln_g), full(ln_b), full(rw_hi), full(rw_lo), full(rb)],
        out_specs=[row_spec(EXT_COLS), pl.BlockSpec((1, rows), lambda i: (0, i))],
        compiler_params=pltpu.CompilerParams(
            dimension_semantics=("arbitrary",), vmem_limit_bytes=VMEM_LIMIT),
        name="mix_route",
    )(mc, ma, h, w_out_bf, ln_g, ln_b, rw_hi, rw_lo, rb)


def _rank_kernel(cls_ref, before_ref, ones_ref, rank_ref, cnt_ref, run_ref):
    i = pl.program_id(0)
    rows = cls_ref.shape[1]

    @pl.when(i == 0)
    def _():
        run_ref[...] = jnp.zeros_like(run_ref)

    cidx = lax.broadcasted_iota(jnp.int32, (LANES, rows), 0)
    onehot = cidx == cls_ref[...]
    oh = jnp.where(onehot, 1.0, 0.0).astype(_BF16)
    within = _dot(oh, before_ref[...])
    run = run_ref[...]
    total = within + jnp.concatenate([run] * (rows // LANES), axis=1)
    rank_ref[...] = jnp.sum(jnp.where(onehot, total, 0.0), axis=0, keepdims=True).astype(jnp.int32)
    run = run + _dot(oh, ones_ref[...])
    run_ref[...] = run
    cnt_ref[...] = run


def _rank(cls):
    t = cls.shape[1]
    rows = RANK_ROWS
    idx = jnp.arange(rows)
    before = (idx[:, None] < idx[None, :]).astype(_BF16)
    ones = jnp.ones((rows, LANES), _BF16)
    return pl.pallas_call(
        _rank_kernel,
        out_shape=[jax.ShapeDtypeStruct((1, t), jnp.int32),
                   jax.ShapeDtypeStruct((LANES, LANES), _F32)],
        grid=(t // rows,),
        in_specs=[pl.BlockSpec((1, rows), lambda i: (0, i)),
                  pl.BlockSpec(before.shape, lambda i: (0, 0)),
                  pl.BlockSpec(ones.shape, lambda i: (0, 0))],
        out_specs=[pl.BlockSpec((1, rows), lambda i: (0, i)),
                   pl.BlockSpec((LANES, LANES), lambda i: (0, 0))],
        scratch_shapes=[pltpu.VMEM((LANES, LANES), _F32)],
        compiler_params=pltpu.CompilerParams(dimension_semantics=("arbitrary",)),
        name="rank",
    )(cls, before, ones)


def _dest_kernel(cls_ref, rank_ref, start_ref, dest_ref):
    rows = cls_ref.shape[1]
    cidx = lax.broadcasted_iota(jnp.int32, (LANES, rows), 0)
    start = jnp.concatenate([start_ref[...]] * (rows // LANES), axis=1)
    base = jnp.sum(jnp.where(cidx == cls_ref[...], start, 0), axis=0, keepdims=True)
    dest_ref[...] = base + rank_ref[...]


def _dest(cls, rank, class_start):
    t = cls.shape[1]
    rows = RANK_ROWS
    start_b = jnp.broadcast_to(
        jnp.pad(class_start, (0, LANES - class_start.shape[0]))[:, None], (LANES, LANES))
    tok_spec = pl.BlockSpec((1, rows), lambda i: (0, i))
    return pl.pallas_call(
        _dest_kernel,
        out_shape=jax.ShapeDtypeStruct((1, t), jnp.int32),
        grid=(t // rows,),
        in_specs=[tok_spec, tok_spec, pl.BlockSpec((LANES, LANES), lambda i: (0, 0))],
        out_specs=tok_spec,
        compiler_params=pltpu.CompilerParams(dimension_semantics=("arbitrary",)),
        name="dest",
    )(cls, rank, start_b)


def _scatter_rows_kernel(dest_ref, src_ref, init_ref, out_ref, sem):
    del init_ref
    rows = dest_ref.shape[0]
    base = pl.program_id(0) * rows

    def issue(r, carry):
        pltpu.make_async_copy(src_ref.at[pl.ds(base + r, 1)],
                              out_ref.at[pl.ds(dest_ref[r], 1)], sem).start()
        return carry

    lax.fori_loop(0, rows, issue, 0, unroll=8)
    pltpu.make_async_copy(src_ref.at[pl.ds(0, rows)], out_ref.at[pl.ds(0, rows)], sem).wait()


def _scatter_rows(dest, src, n_slots):
    t, cols = src.shape
    rows = min(PERM_ROWS, t)
    init = jnp.zeros((n_slots, cols), src.dtype)
    return pl.pallas_call(
        _scatter_rows_kernel,
        out_shape=jax.ShapeDtypeStruct((n_slots, cols), src.dtype),
        grid=(t // rows,),
        in_specs=[pl.BlockSpec((rows,), lambda i: (i,), memory_space=pltpu.SMEM),
                  pl.BlockSpec(memory_space=pl.ANY),
                  pl.BlockSpec(memory_space=pl.ANY)],
        out_specs=pl.BlockSpec(memory_space=pl.ANY),
        scratch_shapes=[pltpu.SemaphoreType.DMA(())],
        input_output_aliases={2: 0},
        compiler_params=pltpu.CompilerParams(dimension_semantics=("arbitrary",)),
        name="scatter_rows",
    )(dest, src, init)


def _gather_rows_kernel(dest_ref, src_ref, out_ref, sem):
    rows = dest_ref.shape[0]
    base = pl.program_id(0) * rows

    def issue(r, carry):
        pltpu.make_async_copy(src_ref.at[pl.ds(dest_ref[r], 1)],
                              out_ref.at[pl.ds(base + r, 1)], sem).start()
        return carry

    lax.fori_loop(0, rows, issue, 0, unroll=8)
    pltpu.make_async_copy(src_ref.at[pl.ds(0, rows)], out_ref.at[pl.ds(0, rows)], sem).wait()


def _gather_rows(dest, src):
    t = dest.shape[0]
    cols = src.shape[1]
    rows = min(PERM_ROWS, t)
    return pl.pallas_call(
        _gather_rows_kernel,
        out_shape=jax.ShapeDtypeStruct((t, cols), src.dtype),
        grid=(t // rows,),
        in_specs=[pl.BlockSpec((rows,), lambda i: (i,), memory_space=pltpu.SMEM),
                  pl.BlockSpec(memory_space=pl.ANY)],
        out_specs=pl.BlockSpec(memory_space=pl.ANY),
        scratch_shapes=[pltpu.SemaphoreType.DMA(())],
        compiler_params=pltpu.CompilerParams(dimension_semantics=("arbitrary",)),
        name="gather_rows",
    )(dest, src)


def _moe_kernel(alpha, ea_ref, eb_ref, xb_ref, nu_ref, xs_ref,
                wga_ref, wua_ref, wda_ref, wgb_ref, wub_ref, wdb_ref, g_ref, b_ref, o_ref):
    del ea_ref, eb_ref, xb_ref

    @pl.when(pl.program_id(0) < nu_ref[0])
    def _():
        x = xs_ref[:, 0:D_MODEL]
        xb = x.astype(_BF16)
        wts = xs_ref[:, D_MODEL:]
        y = None
        for lane, (wg, wu, wd) in enumerate(((wga_ref, wua_ref, wda_ref), (wgb_ref, wub_ref, wdb_ref))):
            a = _dot(xb, wg[...])
            u = _dot(xb, wu[...])
            hid = a / (1.0 + jnp.exp(-a)) * u * wts[:, lane:lane + 1]
            part = _dot(hid.astype(_BF16), wd[...])
            y = part if y is None else y + part
        o_ref[...] = _layer_norm(alpha * x + y, g_ref[...], b_ref[...])

    @pl.when(pl.program_id(0) >= nu_ref[0])
    def _():
        o_ref[...] = jnp.zeros_like(o_ref)


def _moe(alpha, blk_ea, blk_eb, blk_x, n_used, xs, wg, wu, wd, ln_g, ln_b):
    n_slots = xs.shape[0]
    rows = MOE_ROWS
    wspec_a = lambda a: pl.BlockSpec((None,) + a.shape[1:], lambda i, ea, eb, xb, nu: (ea[i], 0, 0))
    wspec_b = lambda a: pl.BlockSpec((None,) + a.shape[1:], lambda i, ea, eb, xb, nu: (eb[i], 0, 0))
    full = lambda a: pl.BlockSpec(a.shape, lambda i, ea, eb, xb, nu: (0,) * a.ndim)
    grid_spec = pltpu.PrefetchScalarGridSpec(
        num_scalar_prefetch=4,
        grid=(n_slots // rows,),
        in_specs=[pl.BlockSpec((rows, EXT_COLS), lambda i, ea, eb, xb, nu: (xb[i], 0)),
                  wspec_a(wg), wspec_a(wu), wspec_a(wd),
                  wspec_b(wg), wspec_b(wu), wspec_b(wd),
                  full(ln_g), full(ln_b)],
        out_specs=pl.BlockSpec((rows, D_MODEL), lambda i, ea, eb, xb, nu: (i, 0)),
    )
    return pl.pallas_call(
        functools.partial(_moe_kernel, alpha),
        out_shape=jax.ShapeDtypeStruct((n_slots, D_MODEL), _F32),
        grid_spec=grid_spec,
        compiler_params=pltpu.CompilerParams(
            dimension_semantics=("arbitrary",), vmem_limit_bytes=VMEM_LIMIT),
        name="moe",
    )(blk_ea, blk_eb, blk_x, n_used, xs, wg, wu, wd, wg, wu, wd, ln_g, ln_b)


def _block_tables(counts, n_blocks):
    padded = (counts + MOE_ROWS - 1) // MOE_ROWS * MOE_ROWS
    ends = jnp.cumsum(padded)
    starts = ends - padded
    n_used = ends[-1] // MOE_ROWS
    blk = jnp.minimum(jnp.arange(n_blocks, dtype=jnp.int32), n_used - 1)
    blk_cls = jnp.clip(jnp.searchsorted(ends, blk * MOE_ROWS, side="right"), 0, N_CLASSES - 1)
    group, pair = blk_cls // N_PAIRS, blk_cls % N_PAIRS
    ea = group * EXPERTS_PER_GROUP + jnp.asarray(PAIR_A, jnp.int32)[pair]
    eb = group * EXPERTS_PER_GROUP + jnp.asarray(PAIR_B, jnp.int32)[pair]
    return (starts.astype(jnp.int32), ea.astype(jnp.int32), eb.astype(jnp.int32),
            blk.astype(jnp.int32), n_used.astype(jnp.int32).reshape(1))


def kernel(x, ln0_g, ln0_b, w_in, conv_w, g_conv, g_attn, w_out, ln_mix_g, ln_mix_b,
           router_w, router_b, w_gate, w_up, w_down, ln_ffn_g, ln_ffn_b):
    bsz, seq, d = x.shape
    depth = w_in.shape[0]
    t = bsz * seq
    alpha = (2.0 * depth) ** 0.25
    row2 = lambda v: v.reshape(1, -1)

    order = jnp.asarray([(r % N_GROUPS) * EXPERTS_PER_GROUP + r // N_GROUPS for r in range(N_EXPERTS)])
    rw = router_w.T[order]
    rw_hi = rw.astype(_BF16)
    rw_lo = (rw - rw_hi.astype(_F32)).astype(_BF16)
    rb = router_b[order].reshape(N_EXPERTS, 1)

    kidx = jnp.arange(ATTN_ROWS)
    ux = jnp.concatenate([(kidx[:, None] > kidx[None, :]).astype(_BF16),
                          jnp.ones((ATTN_ROWS, LANES), _BF16)], axis=1)

    n_slots = (t + N_CLASSES * (MOE_ROWS - 1) + MOE_ROWS - 1) // MOE_ROWS * MOE_ROWS
    n_blocks = n_slots // MOE_ROWS

    h = x.reshape(t, d)
    for l in range(depth):
        w_in_bf = w_in[l].astype(_BF16)
        if l == 0:
            h, mc, qkv = _proj_conv(h, (row2(ln0_g), row2(ln0_b)), w_in_bf, conv_w[l],
                                    row2(g_conv[l]), seq)
        else:
            mc, qkv = _proj_conv(h, None, w_in_bf, conv_w[l], row2(g_conv[l]), seq)
        ma = _attention(qkv.reshape(bsz, seq, 3 * D_ATTN), ux, row2(g_attn[l])).reshape(t, D_ATTN)
        hx, cls = _mix_route(alpha, mc, ma, h, w_out[l].astype(_BF16), row2(ln_mix_g[l]),
                             row2(ln_mix_b[l]), rw_hi, rw_lo, rb)
        rank, cnt = _rank(cls)
        counts = cnt[:N_CLASSES, 0].astype(jnp.int32)
        starts, blk_ea, blk_eb, blk_x, n_used = _block_tables(counts, n_blocks)
        dest = _dest(cls, rank, starts).reshape(t)
        xs = _scatter_rows(dest, hx, n_slots)
        zs = _moe(alpha, blk_ea, blk_eb, blk_x, n_used, xs, w_gate[l].astype(_BF16),
                  w_up[l].astype(_BF16), w_down[l].astype(_BF16),
                  row2(ln_ffn_g[l]), row2(ln_ffn_b[l]))
        h = _gather_rows(dest, zs)
    return h.reshape(bsz, seq, d)
```

```python
import functools
import math

import jax
import jax.numpy as jnp
from jax import lax
from jax.experimental import pallas as pl
from jax.experimental.pallas import tpu as pltpu

D_MODEL = 1024
D_CONV = 512
D_ATTN = 512
HEAD_DIM = 64
N_HEADS = D_ATTN // HEAD_DIM
CONV_K = 3
N_EXPERTS = 32
N_GROUPS = 8
EXPERTS_PER_GROUP = 4
D_EXPERT = 512
LN_EPS = 1e-5
RMS_EPS = 1e-6

PAIR_A = (0, 0, 0, 1, 1, 2)
PAIR_B = (1, 2, 3, 3, 2, 3)
N_PAIRS = len(PAIR_A)
N_CLASSES = N_GROUPS * N_PAIRS

LANES = 128
SUBLANES = 8
ROW_TILES = D_MODEL // LANES
EXT_TILES = 2 * ROW_TILES

PROJ_ROWS = 512
ATTN_ROWS = 256
MIX_ROWS = 256
RANK_ROWS = 512
MOE_ROWS = 128
PLAN_ROWS = 2048

ATTN_STOP = -88.0

VMEM_LIMIT = 56 * 1024 * 1024

_BF16 = jnp.bfloat16
_F32 = jnp.float32


def _dot(a, b):
    return jnp.dot(a, b, preferred_element_type=_F32)


def _dot_nt(a, b):
    return lax.dot_general(a, b, (((1,), (1,)), ((), ())), preferred_element_type=_F32)


def _layer_norm(x, g, b):
    mu = jnp.mean(x, axis=-1, keepdims=True)
    xc = x - mu
    var = jnp.mean(xc * xc, axis=-1, keepdims=True)
    return xc * lax.rsqrt(var + LN_EPS) * g + b


def _rms_norm(x, g):
    ms = jnp.mean(x * x, axis=-1, keepdims=True)
    return x * lax.rsqrt(ms + RMS_EPS) * g


def _load_rows(ref, n_tiles=ROW_TILES):
    return jnp.concatenate([ref[:, j, :] for j in range(n_tiles)], axis=1)


def _store_rows(ref, val):
    for j in range(val.shape[1] // LANES):
        ref[:, j, :] = val[:, j * LANES:(j + 1) * LANES]


def _proj_kernel(apply_ln, tiles_per_seq, *refs):
    if apply_ln:
        x_ref, g0_ref, b0_ref, w_ref, cw_ref, gc_ref, h_ref, mc_ref, qkv_ref, carry_ref = refs
    else:
        x_ref, w_ref, cw_ref, gc_ref, mc_ref, qkv_ref, carry_ref = refs
    i = pl.program_id(0)
    rows = x_ref.shape[0]

    if apply_ln:
        h = _layer_norm(x_ref[...], g0_ref[...], b0_ref[...])
        _store_rows(h_ref, h)
    else:
        h = _load_rows(x_ref)
    hb = h.astype(_BF16)

    cb = _dot(hb, w_ref[:, 0:D_CONV])
    cc = _dot(hb, w_ref[:, D_CONV:2 * D_CONV])
    cx = _dot(hb, w_ref[:, 2 * D_CONV:3 * D_CONV])
    u = cc * cx

    @pl.when(i % tiles_per_seq == 0)
    def _():
        carry_ref[...] = jnp.zeros_like(carry_ref)

    prev = carry_ref[...]
    p1 = prev[SUBLANES - 1:SUBLANES, :]
    p2 = prev[SUBLANES - 2:SUBLANES - 1, :]
    row = lax.broadcasted_iota(jnp.int32, (rows, 1), 0)
    u1 = jnp.where(row == 0, p1, pltpu.roll(u, 1, 0))
    u2 = jnp.where(row == 0, p2, jnp.where(row == 1, p1, pltpu.roll(u, 2, 0)))
    carry_ref[...] = u[rows - SUBLANES:rows, :]

    cw = cw_ref[...]
    conv = cb * (cw[0:1, :] * u2 + cw[1:2, :] * u1 + cw[2:3, :] * u)
    mc_ref[...] = _rms_norm(conv, gc_ref[...]).astype(_BF16)

    qkv = _dot(hb, w_ref[:, 3 * D_CONV:])
    qkv_ref[:, 0:D_ATTN] = (qkv[:, 0:D_ATTN] * (1.0 / math.sqrt(HEAD_DIM))).astype(_BF16)
    qkv_ref[:, D_ATTN:] = qkv[:, D_ATTN:].astype(_BF16)


def _proj_conv(x, ln0, w_in_bf, conv_w, g_conv, seq):
    t = x.shape[0]
    rows = PROJ_ROWS
    apply_ln = ln0 is not None
    in_cols = w_in_bf.shape[1]
    row_spec = lambda c: pl.BlockSpec((rows, c), lambda i: (i, 0))
    tile_spec = pl.BlockSpec((rows, ROW_TILES, LANES), lambda i: (i, 0, 0))
    full = lambda a: pl.BlockSpec(a.shape, lambda i: (0,) * a.ndim)
    args, in_specs = [x], [row_spec(D_MODEL) if apply_ln else tile_spec]
    if apply_ln:
        args += [ln0[0], ln0[1]]
        in_specs += [full(ln0[0]), full(ln0[1])]
    args += [w_in_bf, conv_w, g_conv]
    in_specs += [full(w_in_bf), full(conv_w), full(g_conv)]
    out_shape, out_specs = [], []
    if apply_ln:
        out_shape.append(jax.ShapeDtypeStruct((t, ROW_TILES, LANES), _F32))
        out_specs.append(tile_spec)
    out_shape += [jax.ShapeDtypeStruct((t, D_CONV), _BF16),
                  jax.ShapeDtypeStruct((t, in_cols - 3 * D_CONV), _BF16)]
    out_specs += [row_spec(D_CONV), row_spec(in_cols - 3 * D_CONV)]
    return pl.pallas_call(
        functools.partial(_proj_kernel, apply_ln, seq // rows),
        out_shape=out_shape,
        grid=(t // rows,),
        in_specs=in_specs,
        out_specs=out_specs,
        scratch_shapes=[pltpu.VMEM((SUBLANES, D_CONV), _F32)],
        compiler_params=pltpu.CompilerParams(
            dimension_semantics=("arbitrary",), vmem_limit_bytes=VMEM_LIMIT),
        name="proj_conv",
    )(*args)


def _attn_kernel(q_ref, k_ref, v_ref, ux_ref, ga_ref, o_ref, acc_ref, r_ref):
    i = pl.program_id(1)
    tq = q_ref.shape[0]
    tk = tq

    qpos = lax.broadcasted_iota(jnp.int32, (tq, tk), 0)
    kpos = lax.broadcasted_iota(jnp.int32, (tq, tk), 1)
    causal = kpos < qpos

    def block(j, diagonal):
        ks = pl.multiple_of(j * tk, tk)
        for hd in range(N_HEADS):
            cols = slice(hd * HEAD_DIM, (hd + 1) * HEAD_DIM)
            z = _dot_nt(q_ref[:, cols], k_ref[pl.ds(ks, tk), cols])
            sp = jnp.maximum(z, 0.0) + jnp.log(1.0 + jnp.exp(-jnp.abs(z)))
            log_keep = jnp.where(causal, -sp, 0.0) if diagonal else -sp
            sx = _dot(log_keep.astype(_BF16), ux_ref[...])
            if diagonal:
                arg = z - sp + sx[:, 0:tk]
                a = jnp.where(causal, jnp.exp(arg), 0.0)
                acc_ref[:, cols] = _dot(a.astype(_BF16), v_ref[pl.ds(ks, tk), cols])
                r_ref[hd] = sx[:, tk:]
            else:
                r = r_ref[hd]
                arg = z - sp + sx[:, 0:tk] + jnp.concatenate([r] * (tk // LANES), axis=1)
                a = jnp.exp(arg)
                acc_ref[:, cols] += _dot(a.astype(_BF16), v_ref[pl.ds(ks, tk), cols])
                r_ref[hd] = r + sx[:, tk:]

    block(i, True)

    def cond(j):
        return jnp.logical_and(j >= 0, jnp.max(r_ref[...]) > ATTN_STOP)

    def body(j):
        block(j, False)
        return j - 1

    lax.while_loop(cond, body, i - 1)

    o_ref[...] = _rms_norm(acc_ref[...], ga_ref[...]).astype(_BF16)


def _attention(qkv3, ux, g_attn):
    b, s, _ = qkv3.shape
    tq = ATTN_ROWS
    return pl.pallas_call(
        _attn_kernel,
        out_shape=jax.ShapeDtypeStruct((b, s, D_ATTN), _BF16),
        grid=(b, s // tq),
        in_specs=[
            pl.BlockSpec((None, tq, D_ATTN), lambda bi, i: (bi, i, 0)),
            pl.BlockSpec((None, s, D_ATTN), lambda bi, i: (bi, 0, 1)),
            pl.BlockSpec((None, s, D_ATTN), lambda bi, i: (bi, 0, 2)),
            pl.BlockSpec(ux.shape, lambda bi, i: (0, 0)),
            pl.BlockSpec(g_attn.shape, lambda bi, i: (0, 0)),
        ],
        out_specs=pl.BlockSpec((None, tq, D_ATTN), lambda bi, i: (bi, i, 0)),
        scratch_shapes=[pltpu.VMEM((tq, D_ATTN), _F32),
                        pltpu.VMEM((N_HEADS, tq, LANES), _F32)],
        compiler_params=pltpu.CompilerParams(
            dimension_semantics=("arbitrary", "arbitrary"), vmem_limit_bytes=VMEM_LIMIT),
        name="attention",
    )(qkv3, qkv3, qkv3, ux, g_attn)


def _mix_kernel(alpha, mc_ref, ma_ref, h_ref, wo_ref, g_ref, b_ref, rwh_ref, rwl_ref, rb_ref,
                hx_ref, cls_ref):
    rows = h_ref.shape[0]
    mix = _dot(mc_ref[...], wo_ref[0:D_CONV, :]) + _dot(ma_ref[...], wo_ref[D_CONV:, :])
    h1 = _layer_norm(alpha * _load_rows(h_ref) + mix, g_ref[...], b_ref[...])
    _store_rows(hx_ref, h1)

    hi = h1.astype(_BF16)
    lo = (h1 - hi.astype(_F32)).astype(_BF16)
    logits = (_dot_nt(rwh_ref[...], hi) + _dot_nt(rwh_ref[...], lo) + _dot_nt(rwl_ref[...], hi))
    score = 1.0 / (1.0 + jnp.exp(-logits))
    sel = score + rb_ref[...]
    s = [sel[j * N_GROUPS:(j + 1) * N_GROUPS, :] for j in range(EXPERTS_PER_GROUP)]
    sc = [score[j * N_GROUPS:(j + 1) * N_GROUPS, :] for j in range(EXPERTS_PER_GROUP)]

    gs = None
    for a in range(EXPERTS_PER_GROUP):
        for c in range(a + 1, EXPERTS_PER_GROUP):
            pair = s[a] + s[c]
            gs = pair if gs is None else jnp.maximum(gs, pair)
    gidx = lax.broadcasted_iota(jnp.int32, gs.shape, 0)
    gmax = jnp.max(gs, axis=0, keepdims=True)
    g = jnp.min(jnp.where(gs == gmax, gidx, N_GROUPS), axis=0, keepdims=True)
    in_g = gidx == g
    sv = [jnp.sum(jnp.where(in_g, s[j], 0.0), axis=0, keepdims=True) for j in range(EXPERTS_PER_GROUP)]
    cv = [jnp.sum(jnp.where(in_g, sc[j], 0.0), axis=0, keepdims=True) for j in range(EXPERTS_PER_GROUP)]

    chosen = []
    for j in range(EXPERTS_PER_GROUP):
        rank = jnp.zeros(sv[j].shape, jnp.int32)
        for c in range(EXPERTS_PER_GROUP):
            if c == j:
                continue
            ahead = (sv[c] >= sv[j]) if c < j else (sv[c] > sv[j])
            rank = rank + ahead.astype(jnp.int32)
        chosen.append(rank < 2)
    den = sum(jnp.where(chosen[j], cv[j], 0.0) for j in range(EXPERTS_PER_GROUP))
    wgt = [jnp.where(chosen[j], cv[j] / den, 0.0) for j in range(EXPERTS_PER_GROUP)]
    first = jnp.full(den.shape, EXPERTS_PER_GROUP, jnp.int32)
    last = jnp.full(den.shape, -1, jnp.int32)
    for j in range(EXPERTS_PER_GROUP):
        first = jnp.where(chosen[j], jnp.minimum(first, j), first)
        last = jnp.where(chosen[j], jnp.maximum(last, j), last)
    pair_id = jnp.zeros(den.shape, jnp.int32)
    for p in range(N_PAIRS):
        pair_id = jnp.where(jnp.logical_and(first == PAIR_A[p], last == PAIR_B[p]), p, pair_id)
    w_first = sum(jnp.where(first == j, wgt[j], 0.0) for j in range(EXPERTS_PER_GROUP))
    w_last = sum(jnp.where(last == j, wgt[j], 0.0) for j in range(EXPERTS_PER_GROUP))
    cls_ref[...] = g * N_PAIRS + pair_id

    sub = lax.broadcasted_iota(jnp.int32, (LANES, rows), 0)
    slab = jnp.where(sub == 0, w_first, jnp.where(sub == 1, w_last, 0.0))
    hx_ref[:, ROW_TILES, :] = slab.T
    hx_ref[:, ROW_TILES + 1:, :] = jnp.zeros((rows, EXT_TILES - ROW_TILES - 1, LANES), _F32)


def _mix_route(alpha, mc, ma, h, w_out_bf, ln_g, ln_b, rw_hi, rw_lo, rb):
    t = h.shape[0]
    rows = MIX_ROWS
    row_spec = lambda c: pl.BlockSpec((rows, c), lambda i: (i, 0))
    full = lambda a: pl.BlockSpec(a.shape, lambda i: (0,) * a.ndim)
    return pl.pallas_call(
        functools.partial(_mix_kernel, alpha),
        out_shape=[jax.ShapeDtypeStruct((t, EXT_TILES, LANES), _F32),
                   jax.ShapeDtypeStruct((1, t), jnp.int32)],
        grid=(t // rows,),
        in_specs=[row_spec(D_CONV), row_spec(D_ATTN),
                  pl.BlockSpec((rows, ROW_TILES, LANES), lambda i: (i, 0, 0)), full(w_out_bf),
                  full(ln_g), full(ln_b), full(rw_hi), full(rw_lo), full(rb)],
        out_specs=[pl.BlockSpec((rows, EXT_TILES, LANES), lambda i: (i, 0, 0)),
                   pl.BlockSpec((1, rows), lambda i: (0, i))],
        compiler_params=pltpu.CompilerParams(
            dimension_semantics=("arbitrary",), vmem_limit_bytes=VMEM_LIMIT),
        name="mix_route",
    )(mc, ma, h, w_out_bf, ln_g, ln_b, rw_hi, rw_lo, rb)


def _rank_kernel(cls_ref, before_ref, ones_ref, rank_ref, cnt_ref, run_ref):
    i = pl.program_id(0)
    rows = cls_ref.shape[1]

    @pl.when(i == 0)
    def _():
        run_ref[...] = jnp.zeros_like(run_ref)

    cidx = lax.broadcasted_iota(jnp.int32, (LANES, rows), 0)
    onehot = cidx == cls_ref[...]
    oh = jnp.where(onehot, 1.0, 0.0).astype(_BF16)
    within = _dot(oh, before_ref[...])
    run = run_ref[...]
    total = within + jnp.concatenate([run] * (rows // LANES), axis=1)
    rank_ref[...] = jnp.sum(jnp.where(onehot, total, 0.0), axis=0, keepdims=True).astype(jnp.int32)
    run = run + _dot(oh, ones_ref[...])
    run_ref[...] = run
    cnt_ref[...] = run


def _rank(cls):
    t = cls.shape[1]
    rows = RANK_ROWS
    idx = jnp.arange(rows)
    before = (idx[:, None] < idx[None, :]).astype(_BF16)
    ones = jnp.ones((rows, LANES), _BF16)
    return pl.pallas_call(
        _rank_kernel,
        out_shape=[jax.ShapeDtypeStruct((1, t), jnp.int32),
                   jax.ShapeDtypeStruct((LANES, LANES), _F32)],
        grid=(t // rows,),
        in_specs=[pl.BlockSpec((1, rows), lambda i: (0, i)),
                  pl.BlockSpec(before.shape, lambda i: (0, 0)),
                  pl.BlockSpec(ones.shape, lambda i: (0, 0))],
        out_specs=[pl.BlockSpec((1, rows), lambda i: (0, i)),
                   pl.BlockSpec((LANES, LANES), lambda i: (0, 0))],
        scratch_shapes=[pltpu.VMEM((LANES, LANES), _F32)],
        compiler_params=pltpu.CompilerParams(dimension_semantics=("arbitrary",)),
        name="rank",
    )(cls, before, ones)


def _plan_kernel(cnt_ref, cls_ref, rank_ref,
                 ea_ref, eb_ref, grp_ref, nv_ref, nu_ref, tok_ref, start_ref):
    i = pl.program_id(0)
    chunk = cls_ref.shape[0]
    n_slots = tok_ref.shape[0]
    n_blocks = nv_ref.shape[0]

    @pl.when(i == 0)
    def _():
        def clear(s, carry):
            tok_ref[s] = 0
            return carry
        lax.fori_loop(0, n_slots, clear, 0, unroll=8)

        blk = jnp.int32(0)
        for g in range(N_GROUPS):
            for p in range(N_PAIRS):
                c = g * N_PAIRS + p
                n = cnt_ref[c]
                nb = (n + (MOE_ROWS - 1)) // MOE_ROWS
                start_ref[c] = blk * MOE_ROWS

                def fill(j, carry, blk=blk, n=n, g=g, p=p):
                    ea_ref[blk + j] = PAIR_A[p]
                    eb_ref[blk + j] = PAIR_B[p]
                    grp_ref[blk + j] = g
                    nv_ref[blk + j] = jnp.minimum(n - j * MOE_ROWS, MOE_ROWS)
                    return carry
                lax.fori_loop(0, nb, fill, 0)
                blk = blk + nb
        nu_ref[0] = blk

        def idle(j, carry):
            ea_ref[j] = ea_ref[blk - 1]
            eb_ref[j] = eb_ref[blk - 1]
            grp_ref[j] = grp_ref[blk - 1]
            nv_ref[j] = 0
            return carry
        lax.fori_loop(blk, n_blocks, idle, 0)

    base = i * chunk

    def place(r, carry):
        tok_ref[start_ref[cls_ref[r]] + rank_ref[r]] = base + r
        return carry
    lax.fori_loop(0, chunk, place, 0, unroll=8)


def _plan(counts, cls, rank, n_blocks):
    t = cls.shape[0]
    chunk = min(PLAN_ROWS, t)
    smem = functools.partial(pl.BlockSpec, memory_space=pltpu.SMEM)
    whole = lambda n: smem((n,), lambda i: (0,))
    i32 = lambda n: jax.ShapeDtypeStruct((n,), jnp.int32)
    return pl.pallas_call(
        _plan_kernel,
        out_shape=[i32(n_blocks), i32(n_blocks), i32(n_blocks), i32(n_blocks), i32(1),
                   i32(n_blocks * MOE_ROWS)],
        grid=(t // chunk,),
        in_specs=[whole(counts.shape[0]), smem((chunk,), lambda i: (i,)), smem((chunk,), lambda i: (i,))],
        out_specs=[whole(n_blocks), whole(n_blocks), whole(n_blocks), whole(n_blocks), whole(1),
                   whole(n_blocks * MOE_ROWS)],
        scratch_shapes=[pltpu.SMEM((N_CLASSES,), jnp.int32)],
        compiler_params=pltpu.CompilerParams(dimension_semantics=("arbitrary",)),
        name="plan",
    )(counts, cls, rank)


def _moe_kernel(alpha, ea_ref, eb_ref, grp_ref, nv_ref, nu_ref, tok_ref, tokn_ref, hx_ref,
                wg_ref, wu_ref, wd_ref, g_ref, b_ref, out_ref, xbuf, obuf, gsem, ssem):
    del grp_ref
    b = pl.program_id(0)
    nu = nu_ref[0]
    rows = MOE_ROWS

    def gather_start(tokens, s):
        for r in range(rows):
            pltpu.make_async_copy(hx_ref.at[tokens[r]], xbuf.at[s, r], gsem.at[s]).start()

    def gather_wait(s):
        pltpu.make_async_copy(hx_ref.at[pl.ds(0, rows)], xbuf.at[s], gsem.at[s]).wait()

    def scatter_wait(s, n):
        @pl.when(n > 0)
        def _():
            pltpu.make_async_copy(obuf.at[s, pl.ds(0, n)], out_ref.at[pl.ds(0, n)], ssem.at[s]).wait()

    def step(s):
        o = 1 - s

        @pl.when(b == 0)
        def _():
            gather_start(tok_ref, s)
        gather_wait(s)
        gather_start(tokn_ref, o)

        @pl.when(b >= 2)
        def _():
            scatter_wait(s, nv_ref[jnp.maximum(b - 2, 0)])

        x = _load_rows(xbuf.at[s])
        xb = x.astype(_BF16)
        wts = xbuf[s, :, ROW_TILES, :]
        y = None
        for lane, e_ref in enumerate((ea_ref, eb_ref)):
            e = e_ref[b]
            a = _dot(xb, wg_ref[e])
            u = _dot(xb, wu_ref[e])
            hid = a / (1.0 + jnp.exp(-a)) * u * wts[:, lane:lane + 1]
            part = _dot(hid.astype(_BF16), wd_ref[e])
            y = part if y is None else y + part
        _store_rows(obuf.at[s], _layer_norm(alpha * x + y, g_ref[...], b_ref[...]))

        n_valid = nv_ref[b]
        for r in range(rows):
            @pl.when(r < n_valid)
            def _():
                pltpu.make_async_copy(obuf.at[s, r], out_ref.at[tok_ref[r]], ssem.at[s]).start()

        @pl.when(b == nu - 1)
        def _():
            gather_wait(o)

            @pl.when(b >= 1)
            def _():
                scatter_wait(o, nv_ref[jnp.maximum(b - 1, 0)])
            scatter_wait(s, n_valid)

    for s in range(2):
        pl.when(jnp.logical_and(b < nu, b % 2 == s))(functools.partial(step, s))


def _moe(alpha, ea, eb, grp, nv, nu, tok, hx, wg, wu, wd, ln_g, ln_b):
    t = hx.shape[0]
    n_blocks = nv.shape[0]
    rows = MOE_ROWS
    last = n_blocks - 1
    gspec = lambda a: pl.BlockSpec((None,) + a.shape[1:], lambda i, ea, eb, grp, nv, nu: (grp[i], 0, 0, 0))
    full = lambda a: pl.BlockSpec(a.shape, lambda i, ea, eb, grp, nv, nu: (0,) * a.ndim)
    grid_spec = pltpu.PrefetchScalarGridSpec(
        num_scalar_prefetch=5,
        grid=(n_blocks,),
        in_specs=[pl.BlockSpec((rows,), lambda i, ea, eb, grp, nv, nu: (i,), memory_space=pltpu.SMEM),
                  pl.BlockSpec((rows,), lambda i, ea, eb, grp, nv, nu: (jnp.minimum(i + 1, last),),
                               memory_space=pltpu.SMEM),
                  pl.BlockSpec(memory_space=pl.ANY),
                  gspec(wg), gspec(wu), gspec(wd), full(ln_g), full(ln_b)],
        out_specs=pl.BlockSpec(memory_space=pl.ANY),
        scratch_shapes=[pltpu.VMEM((2, rows, EXT_TILES, LANES), _F32),
                        pltpu.VMEM((2, rows, ROW_TILES, LANES), _F32),
                        pltpu.SemaphoreType.DMA((2,)),
                        pltpu.SemaphoreType.DMA((2,))],
    )
    return pl.pallas_call(
        functools.partial(_moe_kernel, alpha),
        out_shape=jax.ShapeDtypeStruct((t, ROW_TILES, LANES), _F32),
        grid_spec=grid_spec,
        compiler_params=pltpu.CompilerParams(
            dimension_semantics=("arbitrary",), vmem_limit_bytes=VMEM_LIMIT),
        name="moe",
    )(ea, eb, grp, nv, nu, tok, tok, hx, wg, wu, wd, ln_g, ln_b)


def kernel(x, ln0_g, ln0_b, w_in, conv_w, g_conv, g_attn, w_out, ln_mix_g, ln_mix_b,
           router_w, router_b, w_gate, w_up, w_down, ln_ffn_g, ln_ffn_b):
    bsz, seq, d = x.shape
    depth = w_in.shape[0]
    t = bsz * seq
    alpha = (2.0 * depth) ** 0.25
    row2 = lambda v: v.reshape(1, -1)

    order = jnp.asarray([(r % N_GROUPS) * EXPERTS_PER_GROUP + r // N_GROUPS for r in range(N_EXPERTS)])
    rw = router_w.T[order]
    rw_hi = rw.astype(_BF16)
    rw_lo = (rw - rw_hi.astype(_F32)).astype(_BF16)
    rb = router_b[order].reshape(N_EXPERTS, 1)

    kidx = jnp.arange(ATTN_ROWS)
    ux = jnp.concatenate([(kidx[:, None] > kidx[None, :]).astype(_BF16),
                          jnp.ones((ATTN_ROWS, LANES), _BF16)], axis=1)

    n_slots = (t + N_CLASSES * (MOE_ROWS - 1) + MOE_ROWS - 1) // MOE_ROWS * MOE_ROWS
    n_blocks = n_slots // MOE_ROWS

    h = x.reshape(t, d)
    for l in range(depth):
        w_in_bf = w_in[l].astype(_BF16)
        if l == 0:
            h, mc, qkv = _proj_conv(h, (row2(ln0_g), row2(ln0_b)), w_in_bf, conv_w[l],
                                    row2(g_conv[l]), seq)
        else:
            mc, qkv = _proj_conv(h, None, w_in_bf, conv_w[l], row2(g_conv[l]), seq)
        ma = _attention(qkv.reshape(bsz, seq, 3 * D_ATTN), ux, row2(g_attn[l])).reshape(t, D_ATTN)
        hx, cls = _mix_route(alpha, mc, ma, h, w_out[l].astype(_BF16), row2(ln_mix_g[l]),
                             row2(ln_mix_b[l]), rw_hi, rw_lo, rb)
        rank, cnt = _rank(cls)
        counts = cnt[:, 0].astype(jnp.int32)
        ea, eb, grp, nv, nu, tok = _plan(counts, cls.reshape(t), rank.reshape(t), n_blocks)
        by_group = lambda w: w.astype(_BF16).reshape((N_GROUPS, EXPERTS_PER_GROUP) + w.shape[1:])
        h = _moe(alpha, ea, eb, grp, nv, nu, tok, hx, by_group(w_gate[l]), by_group(w_up[l]),
                 by_group(w_down[l]), row2(ln_ffn_g[l]), row2(ln_ffn_b[l]))
    return h.reshape(bsz, seq, d)
```

```python
import functools
import math

import jax
import jax.numpy as jnp
from jax import lax
from jax.experimental import pallas as pl
from jax.experimental.pallas import tpu as pltpu

D_MODEL = 1024
D_CONV = 512
D_ATTN = 512
HEAD_DIM = 64
N_HEADS = D_ATTN // HEAD_DIM
CONV_K = 3
N_EXPERTS = 32
N_GROUPS = 8
EXPERTS_PER_GROUP = 4
D_EXPERT = 512
LN_EPS = 1e-5
RMS_EPS = 1e-6

PAIR_A = (0, 0, 0, 1, 1, 2)
PAIR_B = (1, 2, 3, 3, 2, 3)
N_PAIRS = len(PAIR_A)
N_CLASSES = N_GROUPS * N_PAIRS

LANES = 128
SUBLANES = 8
ROW_TILES = D_MODEL // LANES
EXT_TILES = 2 * ROW_TILES

PROJ_ROWS = 512
ATTN_ROWS = 256
MIX_ROWS = 512
RANK_ROWS = 512
MOE_ROWS = 128
PLAN_ROWS = 2048

ATTN_STOP = -88.0

VMEM_LIMIT = 56 * 1024 * 1024

_BF16 = jnp.bfloat16
_F32 = jnp.float32


def _dot(a, b):
    return jnp.dot(a, b, preferred_element_type=_F32)


def _dot_nt(a, b):
    return lax.dot_general(a, b, (((1,), (1,)), ((), ())), preferred_element_type=_F32)


def _layer_norm(x, g, b):
    mu = jnp.mean(x, axis=-1, keepdims=True)
    xc = x - mu
    var = jnp.mean(xc * xc, axis=-1, keepdims=True)
    return xc * lax.rsqrt(var + LN_EPS) * g + b


def _rms_norm(x, g):
    ms = jnp.mean(x * x, axis=-1, keepdims=True)
    return x * lax.rsqrt(ms + RMS_EPS) * g


def _load_rows(ref, n_tiles=ROW_TILES):
    rows = ref.shape[0]
    return ref[:, 0:n_tiles, :].reshape(rows, n_tiles * LANES)


def _store_rows(ref, val):
    rows, cols = val.shape
    ref[:, 0:cols // LANES, :] = val.reshape(rows, cols // LANES, LANES)


def _proj_kernel(apply_ln, tiles_per_seq, *refs):
    if apply_ln:
        x_ref, g0_ref, b0_ref, w_ref, cw_ref, gc_ref, h_ref, mc_ref, qkv_ref, carry_ref = refs
    else:
        x_ref, w_ref, cw_ref, gc_ref, mc_ref, qkv_ref, carry_ref = refs
    i = pl.program_id(0)
    rows = x_ref.shape[0]

    if apply_ln:
        h = _layer_norm(x_ref[...], g0_ref[...], b0_ref[...])
        _store_rows(h_ref, h)
    else:
        h = _load_rows(x_ref)
    hb = h.astype(_BF16)

    cb = _dot(hb, w_ref[:, 0:D_CONV])
    cc = _dot(hb, w_ref[:, D_CONV:2 * D_CONV])
    cx = _dot(hb, w_ref[:, 2 * D_CONV:3 * D_CONV])
    u = cc * cx

    @pl.when(i % tiles_per_seq == 0)
    def _():
        carry_ref[...] = jnp.zeros_like(carry_ref)

    prev = carry_ref[...]
    p1 = prev[SUBLANES - 1:SUBLANES, :]
    p2 = prev[SUBLANES - 2:SUBLANES - 1, :]
    row = lax.broadcasted_iota(jnp.int32, (rows, 1), 0)
    u1 = jnp.where(row == 0, p1, pltpu.roll(u, 1, 0))
    u2 = jnp.where(row == 0, p2, jnp.where(row == 1, p1, pltpu.roll(u, 2, 0)))
    carry_ref[...] = u[rows - SUBLANES:rows, :]

    cw = cw_ref[...]
    conv = cb * (cw[0:1, :] * u2 + cw[1:2, :] * u1 + cw[2:3, :] * u)
    mc_ref[...] = _rms_norm(conv, gc_ref[...]).astype(_BF16)

    qkv = _dot(hb, w_ref[:, 3 * D_CONV:])
    qkv_ref[:, 0:D_ATTN] = (qkv[:, 0:D_ATTN] * (1.0 / math.sqrt(HEAD_DIM))).astype(_BF16)
    qkv_ref[:, D_ATTN:] = qkv[:, D_ATTN:].astype(_BF16)


def _proj_conv(x, ln0, w_in_bf, conv_w, g_conv, seq):
    t = x.shape[0]
    rows = PROJ_ROWS
    apply_ln = ln0 is not None
    in_cols = w_in_bf.shape[1]
    row_spec = lambda c: pl.BlockSpec((rows, c), lambda i: (i, 0))
    tile_spec = pl.BlockSpec((rows, ROW_TILES, LANES), lambda i: (i, 0, 0))
    full = lambda a: pl.BlockSpec(a.shape, lambda i: (0,) * a.ndim)
    args, in_specs = [x], [row_spec(D_MODEL) if apply_ln else tile_spec]
    if apply_ln:
        args += [ln0[0], ln0[1]]
        in_specs += [full(ln0[0]), full(ln0[1])]
    args += [w_in_bf, conv_w, g_conv]
    in_specs += [full(w_in_bf), full(conv_w), full(g_conv)]
    out_shape, out_specs = [], []
    if apply_ln:
        out_shape.append(jax.ShapeDtypeStruct((t, ROW_TILES, LANES), _F32))
        out_specs.append(tile_spec)
    out_shape += [jax.ShapeDtypeStruct((t, D_CONV), _BF16),
                  jax.ShapeDtypeStruct((t, in_cols - 3 * D_CONV), _BF16)]
    out_specs += [row_spec(D_CONV), row_spec(in_cols - 3 * D_CONV)]
    return pl.pallas_call(
        functools.partial(_proj_kernel, apply_ln, seq // rows),
        out_shape=out_shape,
        grid=(t // rows,),
        in_specs=in_specs,
        out_specs=out_specs,
        scratch_shapes=[pltpu.VMEM((SUBLANES, D_CONV), _F32)],
        compiler_params=pltpu.CompilerParams(
            dimension_semantics=("arbitrary",), vmem_limit_bytes=VMEM_LIMIT),
        name="proj_conv",
    )(*args)


def _attn_kernel(q_ref, k_ref, v_ref, ux_ref, ga_ref, o_ref, acc_ref, r_ref, lk_ref, ls_ref, sx_ref):
    i = pl.program_id(1)
    tq = q_ref.shape[0]
    tk = tq

    qpos = lax.broadcasted_iota(jnp.int32, (tq, tk), 0)
    kpos = lax.broadcasted_iota(jnp.int32, (tq, tk), 1)
    causal = kpos < qpos

    def block(j, diagonal):
        ks = pl.multiple_of(j * tk, tk)
        for hd in range(N_HEADS):
            cols = slice(hd * HEAD_DIM, (hd + 1) * HEAD_DIM)
            z = _dot_nt(q_ref[:, cols], k_ref[pl.ds(ks, tk), cols])
            zb = z.astype(_BF16)
            nsp = jnp.minimum(-zb, 0.0) - jnp.log(1.0 + jnp.exp(-jnp.abs(zb)))
            ls_ref[hd] = z + nsp.astype(_F32)
            lk_ref[hd] = jnp.where(causal, nsp, jnp.zeros_like(nsp)) if diagonal else nsp
        sx_ref[...] = _dot(lk_ref[...].reshape(N_HEADS * tq, tk), ux_ref[...])
        for hd in range(N_HEADS):
            cols = slice(hd * HEAD_DIM, (hd + 1) * HEAD_DIM)
            rows = slice(hd * tq, (hd + 1) * tq)
            if diagonal:
                a = jnp.where(causal, jnp.exp(ls_ref[hd] + sx_ref[rows, 0:tk]), 0.0)
                acc_ref[:, cols] = _dot(a.astype(_BF16), v_ref[pl.ds(ks, tk), cols])
                r_ref[hd] = sx_ref[rows, tk:]
            else:
                r = r_ref[hd]
                arg = ls_ref[hd] + sx_ref[rows, 0:tk] + jnp.concatenate([r] * (tk // LANES), axis=1)
                acc_ref[:, cols] += _dot(jnp.exp(arg).astype(_BF16), v_ref[pl.ds(ks, tk), cols])
                r_ref[hd] = r + sx_ref[rows, tk:]

    block(i, True)

    def cond(j):
        return jnp.logical_and(j >= 0, jnp.max(r_ref[...]) > ATTN_STOP)

    def body(j):
        block(j, False)
        return j - 1

    lax.while_loop(cond, body, i - 1)

    o_ref[...] = _rms_norm(acc_ref[...], ga_ref[...]).astype(_BF16)


def _attention(qkv3, ux, g_attn):
    b, s, _ = qkv3.shape
    tq = ATTN_ROWS
    return pl.pallas_call(
        _attn_kernel,
        out_shape=jax.ShapeDtypeStruct((b, s, D_ATTN), _BF16),
        grid=(b, s // tq),
        in_specs=[
            pl.BlockSpec((None, tq, D_ATTN), lambda bi, i: (bi, i, 0)),
            pl.BlockSpec((None, s, D_ATTN), lambda bi, i: (bi, 0, 1)),
            pl.BlockSpec((None, s, D_ATTN), lambda bi, i: (bi, 0, 2)),
            pl.BlockSpec(ux.shape, lambda bi, i: (0, 0)),
            pl.BlockSpec(g_attn.shape, lambda bi, i: (0, 0)),
        ],
        out_specs=pl.BlockSpec((None, tq, D_ATTN), lambda bi, i: (bi, i, 0)),
        scratch_shapes=[pltpu.VMEM((tq, D_ATTN), _F32),
                        pltpu.VMEM((N_HEADS, tq, LANES), _F32),
                        pltpu.VMEM((N_HEADS, tq, tq), _BF16),
                        pltpu.VMEM((N_HEADS, tq, tq), _F32),
                        pltpu.VMEM((N_HEADS * tq, tq + LANES), _F32)],
        compiler_params=pltpu.CompilerParams(
            dimension_semantics=("arbitrary", "arbitrary"), vmem_limit_bytes=VMEM_LIMIT),
        name="attention",
    )(qkv3, qkv3, qkv3, ux, g_attn)


def _mix_kernel(alpha, mc_ref, ma_ref, h_ref, wo_ref, g_ref, b_ref, rwh_ref, rwl_ref, rb_ref,
                hx_ref, cls_ref):
    rows = h_ref.shape[0]
    mix = _dot(mc_ref[...], wo_ref[0:D_CONV, :]) + _dot(ma_ref[...], wo_ref[D_CONV:, :])
    h1 = _layer_norm(alpha * _load_rows(h_ref) + mix, g_ref[...], b_ref[...])

    hi = h1.astype(_BF16)
    lo = (h1 - hi.astype(_F32)).astype(_BF16)
    logits = (_dot_nt(rwh_ref[...], hi) + _dot_nt(rwh_ref[...], lo) + _dot_nt(rwl_ref[...], hi))
    score = 1.0 / (1.0 + jnp.exp(-logits))
    sel = score + rb_ref[...]
    s = [sel[j * N_GROUPS:(j + 1) * N_GROUPS, :] for j in range(EXPERTS_PER_GROUP)]
    sc = [score[j * N_GROUPS:(j + 1) * N_GROUPS, :] for j in range(EXPERTS_PER_GROUP)]

    gs = None
    for a in range(EXPERTS_PER_GROUP):
        for c in range(a + 1, EXPERTS_PER_GROUP):
            pair = s[a] + s[c]
            gs = pair if gs is None else jnp.maximum(gs, pair)
    gidx = lax.broadcasted_iota(jnp.int32, gs.shape, 0)
    gmax = jnp.max(gs, axis=0, keepdims=True)
    g = jnp.min(jnp.where(gs == gmax, gidx, N_GROUPS), axis=0, keepdims=True)
    in_g = gidx == g
    sv = [jnp.sum(jnp.where(in_g, s[j], 0.0), axis=0, keepdims=True) for j in range(EXPERTS_PER_GROUP)]
    cv = [jnp.sum(jnp.where(in_g, sc[j], 0.0), axis=0, keepdims=True) for j in range(EXPERTS_PER_GROUP)]

    chosen = []
    for j in range(EXPERTS_PER_GROUP):
        rank = jnp.zeros(sv[j].shape, jnp.int32)
        for c in range(EXPERTS_PER_GROUP):
            if c == j:
                continue
            ahead = (sv[c] >= sv[j]) if c < j else (sv[c] > sv[j])
            rank = rank + ahead.astype(jnp.int32)
        chosen.append(rank < 2)
    den = sum(jnp.where(chosen[j], cv[j], 0.0) for j in range(EXPERTS_PER_GROUP))
    wgt = [jnp.where(chosen[j], cv[j] / den, 0.0) for j in range(EXPERTS_PER_GROUP)]
    first = jnp.full(den.shape, EXPERTS_PER_GROUP, jnp.int32)
    last = jnp.full(den.shape, -1, jnp.int32)
    for j in range(EXPERTS_PER_GROUP):
        first = jnp.where(chosen[j], jnp.minimum(first, j), first)
        last = jnp.where(chosen[j], jnp.maximum(last, j), last)
    pair_id = jnp.zeros(den.shape, jnp.int32)
    for p in range(N_PAIRS):
        pair_id = jnp.where(jnp.logical_and(first == PAIR_A[p], last == PAIR_B[p]), p, pair_id)
    w_first = sum(jnp.where(first == j, wgt[j], 0.0) for j in range(EXPERTS_PER_GROUP))
    w_last = sum(jnp.where(last == j, wgt[j], 0.0) for j in range(EXPERTS_PER_GROUP))
    cls_ref[...] = g * N_PAIRS + pair_id

    sub = lax.broadcasted_iota(jnp.int32, (LANES, rows), 0)
    slab = jnp.where(sub == 0, w_first, jnp.where(sub == 1, w_last, 0.0))
    routed = jnp.concatenate(
        [h1, slab.T, jnp.zeros((rows, (EXT_TILES - ROW_TILES - 1) * LANES), _F32)], axis=1)
    _store_rows(hx_ref, routed)


def _mix_route(alpha, mc, ma, h, w_out_bf, ln_g, ln_b, rw_hi, rw_lo, rb):
    t = h.shape[0]
    rows = MIX_ROWS
    row_spec = lambda c: pl.BlockSpec((rows, c), lambda i: (i, 0))
    full = lambda a: pl.BlockSpec(a.shape, lambda i: (0,) * a.ndim)
    return pl.pallas_call(
        functools.partial(_mix_kernel, alpha),
        out_shape=[jax.ShapeDtypeStruct((t, EXT_TILES, LANES), _F32),
                   jax.ShapeDtypeStruct((1, t), jnp.int32)],
        grid=(t // rows,),
        in_specs=[row_spec(D_CONV), row_spec(D_ATTN),
                  pl.BlockSpec((rows, ROW_TILES, LANES), lambda i: (i, 0, 0)), full(w_out_bf),
                  full(ln_g), full(ln_b), full(rw_hi), full(rw_lo), full(rb)],
        out_specs=[pl.BlockSpec((rows, EXT_TILES, LANES), lambda i: (i, 0, 0)),
                   pl.BlockSpec((1, rows), lambda i: (0, i))],
        compiler_params=pltpu.CompilerParams(
            dimension_semantics=("arbitrary",), vmem_limit_bytes=VMEM_LIMIT),
        name="mix_route",
    )(mc, ma, h, w_out_bf, ln_g, ln_b, rw_hi, rw_lo, rb)


def _rank_kernel(cls_ref, before_ref, ones_ref, rank_ref, cnt_ref, run_ref):
    i = pl.program_id(0)
    rows = cls_ref.shape[1]

    @pl.when(i == 0)
    def _():
        run_ref[...] = jnp.zeros_like(run_ref)

    cidx = lax.broadcasted_iota(jnp.int32, (LANES, rows), 0)
    onehot = cidx == cls_ref[...]
    oh = jnp.where(onehot, 1.0, 0.0).astype(_BF16)
    within = _dot(oh, before_ref[...])
    run = run_ref[...]
    total = within + jnp.concatenate([run] * (rows // LANES), axis=1)
    rank_ref[...] = jnp.sum(jnp.where(onehot, total, 0.0), axis=0, keepdims=True).astype(jnp.int32)
    run = run + _dot(oh, ones_ref[...])
    run_ref[...] = run
    cnt_ref[...] = run


def _rank(cls):
    t = cls.shape[1]
    rows = RANK_ROWS
    idx = jnp.arange(rows)
    before = (idx[:, None] < idx[None, :]).astype(_BF16)
    ones = jnp.ones((rows, LANES), _BF16)
    return pl.pallas_call(
        _rank_kernel,
        out_shape=[jax.ShapeDtypeStruct((1, t), jnp.int32),
                   jax.ShapeDtypeStruct((LANES, LANES), _F32)],
        grid=(t // rows,),
        in_specs=[pl.BlockSpec((1, rows), lambda i: (0, i)),
                  pl.BlockSpec(before.shape, lambda i: (0, 0)),
                  pl.BlockSpec(ones.shape, lambda i: (0, 0))],
        out_specs=[pl.BlockSpec((1, rows), lambda i: (0, i)),
                   pl.BlockSpec((LANES, LANES), lambda i: (0, 0))],
        scratch_shapes=[pltpu.VMEM((LANES, LANES), _F32)],
        compiler_params=pltpu.CompilerParams(dimension_semantics=("arbitrary",)),
        name="rank",
    )(cls, before, ones)


def _plan_kernel(cnt_ref, cls_ref, rank_ref,
                 ea_ref, eb_ref, grp_ref, nv_ref, nu_ref, tok_ref, start_ref):
    i = pl.program_id(0)
    chunk = cls_ref.shape[0]
    n_slots = tok_ref.shape[0]
    n_blocks = nv_ref.shape[0]

    @pl.when(i == 0)
    def _():
        def pad(s, carry):
            tok_ref[s] = 0
            return carry

        blk = jnp.int32(0)
        for g in range(N_GROUPS):
            for p in range(N_PAIRS):
                c = g * N_PAIRS + p
                n = cnt_ref[c]
                nb = (n + (MOE_ROWS - 1)) // MOE_ROWS
                start = blk * MOE_ROWS
                start_ref[c] = start

                def fill(j, carry, blk=blk, n=n, g=g, p=p):
                    ea_ref[blk + j] = PAIR_A[p]
                    eb_ref[blk + j] = PAIR_B[p]
                    grp_ref[blk + j] = g
                    nv_ref[blk + j] = jnp.minimum(n - j * MOE_ROWS, MOE_ROWS)
                    return carry
                lax.fori_loop(0, nb, fill, 0)
                lax.fori_loop(start + n, start + nb * MOE_ROWS, pad, 0)
                blk = blk + nb
        nu_ref[0] = blk

        def idle(j, carry):
            ea_ref[j] = ea_ref[blk - 1]
            eb_ref[j] = eb_ref[blk - 1]
            grp_ref[j] = grp_ref[blk - 1]
            nv_ref[j] = 0
            return carry
        lax.fori_loop(blk, n_blocks, idle, 0)
        lax.fori_loop(blk * MOE_ROWS, n_slots, pad, 0)

    base = i * chunk

    def place(r, carry):
        tok_ref[start_ref[cls_ref[r]] + rank_ref[r]] = base + r
        return carry
    lax.fori_loop(0, chunk, place, 0, unroll=8)


def _plan(counts, cls, rank, n_blocks):
    t = cls.shape[0]
    chunk = min(PLAN_ROWS, t)
    smem = functools.partial(pl.BlockSpec, memory_space=pltpu.SMEM)
    whole = lambda n: smem((n,), lambda i: (0,))
    i32 = lambda n: jax.ShapeDtypeStruct((n,), jnp.int32)
    return pl.pallas_call(
        _plan_kernel,
        out_shape=[i32(n_blocks), i32(n_blocks), i32(n_blocks), i32(n_blocks), i32(1),
                   i32(n_blocks * MOE_ROWS)],
        grid=(t // chunk,),
        in_specs=[whole(counts.shape[0]), smem((chunk,), lambda i: (i,)), smem((chunk,), lambda i: (i,))],
        out_specs=[whole(n_blocks), whole(n_blocks), whole(n_blocks), whole(n_blocks), whole(1),
                   whole(n_blocks * MOE_ROWS)],
        scratch_shapes=[pltpu.SMEM((N_CLASSES,), jnp.int32)],
        compiler_params=pltpu.CompilerParams(dimension_semantics=("arbitrary",)),
        name="plan",
    )(counts, cls, rank)


def _moe_kernel(alpha, ea_ref, eb_ref, grp_ref, nv_ref, nu_ref, tok_ref, tokn_ref, hx_ref,
                wg_ref, wu_ref, wd_ref, g_ref, b_ref, out_ref, xbuf, obuf, gsem, ssem):
    del grp_ref
    b = pl.program_id(0)
    nu = nu_ref[0]
    rows = MOE_ROWS

    def gather_start(tokens, s):
        for r in range(rows):
            pltpu.make_async_copy(hx_ref.at[tokens[r]], xbuf.at[s, r], gsem.at[s]).start()

    def gather_wait(s):
        pltpu.make_async_copy(hx_ref.at[pl.ds(0, rows)], xbuf.at[s], gsem.at[s]).wait()

    def scatter_wait(s, n):
        @pl.when(n > 0)
        def _():
            pltpu.make_async_copy(obuf.at[s, pl.ds(0, n)], out_ref.at[pl.ds(0, n)], ssem.at[s]).wait()

    def step(s):
        o = 1 - s

        @pl.when(b == 0)
        def _():
            gather_start(tok_ref, s)
        gather_wait(s)
        gather_start(tokn_ref, o)

        @pl.when(b >= 2)
        def _():
            scatter_wait(s, nv_ref[jnp.maximum(b - 2, 0)])

        xe = _load_rows(xbuf.at[s], ROW_TILES + SUBLANES)
        x = xe[:, 0:D_MODEL]
        xb = x.astype(_BF16)
        wts = xe[:, D_MODEL:D_MODEL + LANES]
        y = None
        for lane, e_ref in enumerate((ea_ref, eb_ref)):
            e = e_ref[b]
            a = _dot(xb, wg_ref[e])
            u = _dot(xb, wu_ref[e])
            hid = a / (1.0 + jnp.exp(-a)) * u * wts[:, lane:lane + 1]
            part = _dot(hid.astype(_BF16), wd_ref[e])
            y = part if y is None else y + part
        _store_rows(obuf.at[s], _layer_norm(alpha * x + y, g_ref[...], b_ref[...]))

        n_valid = nv_ref[b]

        @pl.when(n_valid == rows)
        def _():
            for r in range(rows):
                pltpu.make_async_copy(obuf.at[s, r], out_ref.at[tok_ref[r]], ssem.at[s]).start()

        @pl.when(n_valid < rows)
        def _():
            for r in range(rows):
                @pl.when(r < n_valid)
                def _():
                    pltpu.make_async_copy(obuf.at[s, r], out_ref.at[tok_ref[r]], ssem.at[s]).start()

        @pl.when(b == nu - 1)
        def _():
            gather_wait(o)

            @pl.when(b >= 1)
            def _():
                scatter_wait(o, nv_ref[jnp.maximum(b - 1, 0)])
            scatter_wait(s, n_valid)

    for s in range(2):
        pl.when(jnp.logical_and(b < nu, b % 2 == s))(functools.partial(step, s))


def _moe(alpha, ea, eb, grp, nv, nu, tok, hx, wg, wu, wd, ln_g, ln_b):
    t = hx.shape[0]
    n_blocks = nv.shape[0]
    rows = MOE_ROWS
    last = n_blocks - 1
    gspec = lambda a: pl.BlockSpec((None,) + a.shape[1:], lambda i, ea, eb, grp, nv, nu: (grp[i], 0, 0, 0))
    full = lambda a: pl.BlockSpec(a.shape, lambda i, ea, eb, grp, nv, nu: (0,) * a.ndim)
    grid_spec = pltpu.PrefetchScalarGridSpec(
        num_scalar_prefetch=5,
        grid=(n_blocks,),
        in_specs=[pl.BlockSpec((rows,), lambda i, ea, eb, grp, nv, nu: (i,), memory_space=pltpu.SMEM),
                  pl.BlockSpec((rows,), lambda i, ea, eb, grp, nv, nu: (jnp.minimum(i + 1, last),),
                               memory_space=pltpu.SMEM),
                  pl.BlockSpec(memory_space=pl.ANY),
                  gspec(wg), gspec(wu), gspec(wd), full(ln_g), full(ln_b)],
        out_specs=pl.BlockSpec(memory_space=pl.ANY),
        scratch_shapes=[pltpu.VMEM((2, rows, EXT_TILES, LANES), _F32),
                        pltpu.VMEM((2, rows, ROW_TILES, LANES), _F32),
                        pltpu.SemaphoreType.DMA((2,)),
                        pltpu.SemaphoreType.DMA((2,))],
    )
    return pl.pallas_call(
        functools.partial(_moe_kernel, alpha),
        out_shape=jax.ShapeDtypeStruct((t, ROW_TILES, LANES), _F32),
        grid_spec=grid_spec,
        compiler_params=pltpu.CompilerParams(
            dimension_semantics=("arbitrary",), vmem_limit_bytes=VMEM_LIMIT),
        name="moe",
    )(ea, eb, grp, nv, nu, tok, tok, hx, wg, wu, wd, ln_g, ln_b)


def kernel(x, ln0_g, ln0_b, w_in, conv_w, g_conv, g_attn, w_out, ln_mix_g, ln_mix_b,
           router_w, router_b, w_gate, w_up, w_down, ln_ffn_g, ln_ffn_b):
    bsz, seq, d = x.shape
    depth = w_in.shape[0]
    t = bsz * seq
    alpha = (2.0 * depth) ** 0.25
    row2 = lambda v: v.reshape(1, -1)

    order = jnp.asarray([(r % N_GROUPS) * EXPERTS_PER_GROUP + r // N_GROUPS for r in range(N_EXPERTS)])
    rw = router_w.T[order]
    rw_hi = rw.astype(_BF16)
    rw_lo = (rw - rw_hi.astype(_F32)).astype(_BF16)
    rb = router_b[order].reshape(N_EXPERTS, 1)

    kidx = jnp.arange(ATTN_ROWS)
    ux = jnp.concatenate([(kidx[:, None] > kidx[None, :]).astype(_BF16),
                          jnp.ones((ATTN_ROWS, LANES), _BF16)], axis=1)

    n_slots = (t + N_CLASSES * (MOE_ROWS - 1) + MOE_ROWS - 1) // MOE_ROWS * MOE_ROWS
    n_blocks = n_slots // MOE_ROWS

    h = x.reshape(t, d)
    for l in range(depth):
        w_in_bf = w_in[l].astype(_BF16)
        if l == 0:
            h, mc, qkv = _proj_conv(h, (row2(ln0_g), row2(ln0_b)), w_in_bf, conv_w[l],
                                    row2(g_conv[l]), seq)
        else:
            mc, qkv = _proj_conv(h, None, w_in_bf, conv_w[l], row2(g_conv[l]), seq)
        ma = _attention(qkv.reshape(bsz, seq, 3 * D_ATTN), ux, row2(g_attn[l])).reshape(t, D_ATTN)
        hx, cls = _mix_route(alpha, mc, ma, h, w_out[l].astype(_BF16), row2(ln_mix_g[l]),
                             row2(ln_mix_b[l]), rw_hi, rw_lo, rb)
        rank, cnt = _rank(cls)
        counts = cnt[:, 0].astype(jnp.int32)
        ea, eb, grp, nv, nu, tok = _plan(counts, cls.reshape(t), rank.reshape(t), n_blocks)
        by_group = lambda w: w.astype(_BF16).reshape((N_GROUPS, EXPERTS_PER_GROUP) + w.shape[1:])
        h = _moe(alpha, ea, eb, grp, nv, nu, tok, hx, by_group(w_gate[l]), by_group(w_up[l]),
                 by_group(w_down[l]), row2(ln_ffn_g[l]), row2(ln_ffn_b[l]))
    return h.reshape(bsz, seq, d)
```

```python
import functools
import math

import jax
import jax.numpy as jnp
from jax import lax
from jax.experimental import pallas as pl
from jax.experimental.pallas import tpu as pltpu

D_MODEL = 1024
D_CONV = 512
D_ATTN = 512
HEAD_DIM = 64
N_HEADS = D_ATTN // HEAD_DIM
CONV_K = 3
N_EXPERTS = 32
N_GROUPS = 8
EXPERTS_PER_GROUP = 4
D_EXPERT = 512
LN_EPS = 1e-5
RMS_EPS = 1e-6

PAIR_A = (0, 0, 0, 1, 1, 2)
PAIR_B = (1, 2, 3, 3, 2, 3)
N_PAIRS = len(PAIR_A)
N_CLASSES = N_GROUPS * N_PAIRS

LANES = 128
SUBLANES = 8
ROW_TILES = D_MODEL // LANES
EXT_TILES = 2 * ROW_TILES

PROJ_ROWS = 1024
ATTN_ROWS = 256
MIX_ROWS = 1024
RANK_ROWS = 512
MOE_ROWS = 128
PLAN_ROWS = 2048

ATTN_STOP = -88.0

VMEM_LIMIT = 56 * 1024 * 1024

_BF16 = jnp.bfloat16
_F32 = jnp.float32


def _dot(a, b):
    return jnp.dot(a, b, preferred_element_type=_F32)


def _dot_nt(a, b):
    return lax.dot_general(a, b, (((1,), (1,)), ((), ())), preferred_element_type=_F32)


def _layer_norm(x, g, b):
    mu = jnp.mean(x, axis=-1, keepdims=True)
    xc = x - mu
    var = jnp.mean(xc * xc, axis=-1, keepdims=True)
    return xc * lax.rsqrt(var + LN_EPS) * g + b


def _rms_norm(x, g):
    ms = jnp.mean(x * x, axis=-1, keepdims=True)
    return x * lax.rsqrt(ms + RMS_EPS) * g


def _load_rows(ref, n_tiles=ROW_TILES):
    rows = ref.shape[0]
    return ref[:, 0:n_tiles, :].reshape(rows, n_tiles * LANES)


def _store_rows(ref, val):
    rows, cols = val.shape
    ref[:, 0:cols // LANES, :] = val.reshape(rows, cols // LANES, LANES)


def _proj_kernel(apply_ln, tiles_per_seq, *refs):
    if apply_ln:
        x_ref, g0_ref, b0_ref, w_ref, cw_ref, gc_ref, h_ref, mc_ref, qkv_ref, carry_ref = refs
    else:
        x_ref, w_ref, cw_ref, gc_ref, mc_ref, qkv_ref, carry_ref = refs
    i = pl.program_id(0)
    rows = x_ref.shape[0]

    if apply_ln:
        h = _layer_norm(x_ref[...], g0_ref[...], b0_ref[...])
        _store_rows(h_ref, h)
    else:
        h = _load_rows(x_ref)
    hb = h.astype(_BF16)

    cb = _dot(hb, w_ref[:, 0:D_CONV])
    cc = _dot(hb, w_ref[:, D_CONV:2 * D_CONV])
    cx = _dot(hb, w_ref[:, 2 * D_CONV:3 * D_CONV])
    u = cc * cx

    @pl.when(i % tiles_per_seq == 0)
    def _():
        carry_ref[...] = jnp.zeros_like(carry_ref)

    prev = carry_ref[...]
    p1 = prev[SUBLANES - 1:SUBLANES, :]
    p2 = prev[SUBLANES - 2:SUBLANES - 1, :]
    row = lax.broadcasted_iota(jnp.int32, (rows, 1), 0)
    u1 = jnp.where(row == 0, p1, pltpu.roll(u, 1, 0))
    u2 = jnp.where(row == 0, p2, jnp.where(row == 1, p1, pltpu.roll(u, 2, 0)))
    carry_ref[...] = u[rows - SUBLANES:rows, :]

    cw = cw_ref[...]
    conv = cb * (cw[0:1, :] * u2 + cw[1:2, :] * u1 + cw[2:3, :] * u)
    mc_ref[...] = _rms_norm(conv, gc_ref[...]).astype(_BF16)

    qkv = _dot(hb, w_ref[:, 3 * D_CONV:])
    qkv_ref[:, 0:D_ATTN] = (qkv[:, 0:D_ATTN] * (1.0 / math.sqrt(HEAD_DIM))).astype(_BF16)
    qkv_ref[:, D_ATTN:] = qkv[:, D_ATTN:].astype(_BF16)


def _proj_conv(x, ln0, w_in_bf, conv_w, g_conv, seq):
    t = x.shape[0]
    rows = PROJ_ROWS
    apply_ln = ln0 is not None
    in_cols = w_in_bf.shape[1]
    row_spec = lambda c: pl.BlockSpec((rows, c), lambda i: (i, 0))
    tile_spec = pl.BlockSpec((rows, ROW_TILES, LANES), lambda i: (i, 0, 0))
    full = lambda a: pl.BlockSpec(a.shape, lambda i: (0,) * a.ndim)
    args, in_specs = [x], [row_spec(D_MODEL) if apply_ln else tile_spec]
    if apply_ln:
        args += [ln0[0], ln0[1]]
        in_specs += [full(ln0[0]), full(ln0[1])]
    args += [w_in_bf, conv_w, g_conv]
    in_specs += [full(w_in_bf), full(conv_w), full(g_conv)]
    out_shape, out_specs = [], []
    if apply_ln:
        out_shape.append(jax.ShapeDtypeStruct((t, ROW_TILES, LANES), _F32))
        out_specs.append(tile_spec)
    out_shape += [jax.ShapeDtypeStruct((t, D_CONV), _BF16),
                  jax.ShapeDtypeStruct((t, in_cols - 3 * D_CONV), _BF16)]
    out_specs += [row_spec(D_CONV), row_spec(in_cols - 3 * D_CONV)]
    return pl.pallas_call(
        functools.partial(_proj_kernel, apply_ln, seq // rows),
        out_shape=out_shape,
        grid=(t // rows,),
        in_specs=in_specs,
        out_specs=out_specs,
        scratch_shapes=[pltpu.VMEM((SUBLANES, D_CONV), _F32)],
        compiler_params=pltpu.CompilerParams(
            dimension_semantics=("arbitrary",), vmem_limit_bytes=VMEM_LIMIT),
        name="proj_conv",
    )(*args)


def _attn_kernel(q_ref, k_ref, v_ref, ux_ref, ga_ref, o_ref, acc_ref, r_ref, lk_ref, ls_ref, sx_ref):
    i = pl.program_id(1)
    tq = q_ref.shape[0]
    tk = tq

    qpos = lax.broadcasted_iota(jnp.int32, (tq, tk), 0)
    kpos = lax.broadcasted_iota(jnp.int32, (tq, tk), 1)
    causal = kpos < qpos

    def block(j, diagonal):
        ks = pl.multiple_of(j * tk, tk)
        for hd in range(N_HEADS):
            cols = slice(hd * HEAD_DIM, (hd + 1) * HEAD_DIM)
            z = _dot_nt(q_ref[:, cols], k_ref[pl.ds(ks, tk), cols])
            zb = z.astype(_BF16)
            nsp = jnp.minimum(-zb, 0.0) - jnp.log(1.0 + jnp.exp(-jnp.abs(zb)))
            ls_ref[hd] = z + nsp.astype(_F32)
            lk_ref[hd] = jnp.where(causal, nsp, jnp.zeros_like(nsp)) if diagonal else nsp
        sx_ref[...] = _dot(lk_ref[...].reshape(N_HEADS * tq, tk), ux_ref[...])
        for hd in range(N_HEADS):
            cols = slice(hd * HEAD_DIM, (hd + 1) * HEAD_DIM)
            rows = slice(hd * tq, (hd + 1) * tq)
            total = jnp.broadcast_to(
                sx_ref[rows, 0:1] + lk_ref[hd, :, 0:1].astype(_F32), (tq, LANES))
            if diagonal:
                a = jnp.where(causal, jnp.exp(ls_ref[hd] + sx_ref[rows, :]), 0.0)
                acc_ref[:, cols] = _dot(a.astype(_BF16), v_ref[pl.ds(ks, tk), cols])
                r_ref[hd] = total
            else:
                r = r_ref[hd]
                arg = ls_ref[hd] + sx_ref[rows, :] + jnp.concatenate([r] * (tk // LANES), axis=1)
                acc_ref[:, cols] += _dot(jnp.exp(arg).astype(_BF16), v_ref[pl.ds(ks, tk), cols])
                r_ref[hd] = r + total

    block(i, True)

    def cond(j):
        return jnp.logical_and(j >= 0, jnp.max(r_ref[...]) > ATTN_STOP)

    def body(j):
        block(j, False)
        return j - 1

    lax.while_loop(cond, body, i - 1)

    o_ref[...] = _rms_norm(acc_ref[...], ga_ref[...]).astype(_BF16)


def _attention(qkv3, ux, g_attn):
    b, s, _ = qkv3.shape
    tq = ATTN_ROWS
    return pl.pallas_call(
        _attn_kernel,
        out_shape=jax.ShapeDtypeStruct((b, s, D_ATTN), _BF16),
        grid=(b, s // tq),
        in_specs=[
            pl.BlockSpec((None, tq, D_ATTN), lambda bi, i: (bi, i, 0)),
            pl.BlockSpec((None, s, D_ATTN), lambda bi, i: (bi, 0, 1)),
            pl.BlockSpec((None, s, D_ATTN), lambda bi, i: (bi, 0, 2)),
            pl.BlockSpec(ux.shape, lambda bi, i: (0, 0)),
            pl.BlockSpec(g_attn.shape, lambda bi, i: (0, 0)),
        ],
        out_specs=pl.BlockSpec((None, tq, D_ATTN), lambda bi, i: (bi, i, 0)),
        scratch_shapes=[pltpu.VMEM((tq, D_ATTN), _F32),
                        pltpu.VMEM((N_HEADS, tq, LANES), _F32),
                        pltpu.VMEM((N_HEADS, tq, tq), _BF16),
                        pltpu.VMEM((N_HEADS, tq, tq), _F32),
                        pltpu.VMEM((N_HEADS * tq, tq), _F32)],
        compiler_params=pltpu.CompilerParams(
            dimension_semantics=("arbitrary", "arbitrary"), vmem_limit_bytes=VMEM_LIMIT),
        name="attention",
    )(qkv3, qkv3, qkv3, ux, g_attn)


def _mix_kernel(alpha, mc_ref, ma_ref, h_ref, wo_ref, g_ref, b_ref, rwh_ref, rwl_ref, rb_ref,
                hx_ref, cls_ref):
    rows = h_ref.shape[0]
    mix = _dot(mc_ref[...], wo_ref[0:D_CONV, :]) + _dot(ma_ref[...], wo_ref[D_CONV:, :])
    h1 = _layer_norm(alpha * _load_rows(h_ref) + mix, g_ref[...], b_ref[...])

    hi = h1.astype(_BF16)
    lo = (h1 - hi.astype(_F32)).astype(_BF16)
    logits = (_dot_nt(rwh_ref[...], hi) + _dot_nt(rwh_ref[...], lo) + _dot_nt(rwl_ref[...], hi))
    score = 1.0 / (1.0 + jnp.exp(-logits))
    sel = score + rb_ref[...]
    s = [sel[j * N_GROUPS:(j + 1) * N_GROUPS, :] for j in range(EXPERTS_PER_GROUP)]
    sc = [score[j * N_GROUPS:(j + 1) * N_GROUPS, :] for j in range(EXPERTS_PER_GROUP)]

    gs = None
    for a in range(EXPERTS_PER_GROUP):
        for c in range(a + 1, EXPERTS_PER_GROUP):
            pair = s[a] + s[c]
            gs = pair if gs is None else jnp.maximum(gs, pair)
    gidx = lax.broadcasted_iota(jnp.int32, gs.shape, 0)
    gmax = jnp.max(gs, axis=0, keepdims=True)
    g = jnp.min(jnp.where(gs == gmax, gidx, N_GROUPS), axis=0, keepdims=True)
    in_g = gidx == g
    sv = [jnp.sum(jnp.where(in_g, s[j], 0.0), axis=0, keepdims=True) for j in range(EXPERTS_PER_GROUP)]
    cv = [jnp.sum(jnp.where(in_g, sc[j], 0.0), axis=0, keepdims=True) for j in range(EXPERTS_PER_GROUP)]

    chosen = []
    for j in range(EXPERTS_PER_GROUP):
        rank = jnp.zeros(sv[j].shape, jnp.int32)
        for c in range(EXPERTS_PER_GROUP):
            if c == j:
                continue
            ahead = (sv[c] >= sv[j]) if c < j else (sv[c] > sv[j])
            rank = rank + ahead.astype(jnp.int32)
        chosen.append(rank < 2)
    den = sum(jnp.where(chosen[j], cv[j], 0.0) for j in range(EXPERTS_PER_GROUP))
    wgt = [jnp.where(chosen[j], cv[j] / den, 0.0) for j in range(EXPERTS_PER_GROUP)]
    first = jnp.full(den.shape, EXPERTS_PER_GROUP, jnp.int32)
    last = jnp.full(den.shape, -1, jnp.int32)
    for j in range(EXPERTS_PER_GROUP):
        first = jnp.where(chosen[j], jnp.minimum(first, j), first)
        last = jnp.where(chosen[j], jnp.maximum(last, j), last)
    pair_id = jnp.zeros(den.shape, jnp.int32)
    for p in range(N_PAIRS):
        pair_id = jnp.where(jnp.logical_and(first == PAIR_A[p], last == PAIR_B[p]), p, pair_id)
    w_first = sum(jnp.where(first == j, wgt[j], 0.0) for j in range(EXPERTS_PER_GROUP))
    w_last = sum(jnp.where(last == j, wgt[j], 0.0) for j in range(EXPERTS_PER_GROUP))
    cls_ref[...] = g * N_PAIRS + pair_id

    sub = lax.broadcasted_iota(jnp.int32, (LANES, rows), 0)
    slab = jnp.where(sub == 0, w_first, jnp.where(sub == 1, w_last, 0.0))
    routed = jnp.concatenate(
        [h1, slab.T, jnp.zeros((rows, (EXT_TILES - ROW_TILES - 1) * LANES), _F32)], axis=1)
    _store_rows(hx_ref, routed)


def _mix_route(alpha, mc, ma, h, w_out_bf, ln_g, ln_b, rw_hi, rw_lo, rb):
    t = h.shape[0]
    rows = MIX_ROWS
    row_spec = lambda c: pl.BlockSpec((rows, c), lambda i: (i, 0))
    full = lambda a: pl.BlockSpec(a.shape, lambda i: (0,) * a.ndim)
    return pl.pallas_call(
        functools.partial(_mix_kernel, alpha),
        out_shape=[jax.ShapeDtypeStruct((t, EXT_TILES, LANES), _F32),
                   jax.ShapeDtypeStruct((1, t), jnp.int32)],
        grid=(t // rows,),
        in_specs=[row_spec(D_CONV), row_spec(D_ATTN),
                  pl.BlockSpec((rows, ROW_TILES, LANES), lambda i: (i, 0, 0)), full(w_out_bf),
                  full(ln_g), full(ln_b), full(rw_hi), full(rw_lo), full(rb)],
        out_specs=[pl.BlockSpec((rows, EXT_TILES, LANES), lambda i: (i, 0, 0)),
                   pl.BlockSpec((1, rows), lambda i: (0, i))],
        compiler_params=pltpu.CompilerParams(
            dimension_semantics=("arbitrary",), vmem_limit_bytes=VMEM_LIMIT),
        name="mix_route",
    )(mc, ma, h, w_out_bf, ln_g, ln_b, rw_hi, rw_lo, rb)


def _rank_kernel(cls_ref, before_ref, ones_ref, rank_ref, cnt_ref, run_ref):
    i = pl.program_id(0)
    rows = cls_ref.shape[1]

    @pl.when(i == 0)
    def _():
        run_ref[...] = jnp.zeros_like(run_ref)

    cidx = lax.broadcasted_iota(jnp.int32, (LANES, rows), 0)
    onehot = cidx == cls_ref[...]
    oh = jnp.where(onehot, 1.0, 0.0).astype(_BF16)
    within = _dot(oh, before_ref[...])
    run = run_ref[...]
    total = within + jnp.concatenate([run] * (rows // LANES), axis=1)
    rank_ref[...] = jnp.sum(jnp.where(onehot, total, 0.0), axis=0, keepdims=True).astype(jnp.int32)
    run = run + _dot(oh, ones_ref[...])
    run_ref[...] = run
    cnt_ref[...] = run


def _rank(cls):
    t = cls.shape[1]
    rows = RANK_ROWS
    idx = jnp.arange(rows)
    before = (idx[:, None] < idx[None, :]).astype(_BF16)
    ones = jnp.ones((rows, LANES), _BF16)
    return pl.pallas_call(
        _rank_kernel,
        out_shape=[jax.ShapeDtypeStruct((1, t), jnp.int32),
                   jax.ShapeDtypeStruct((LANES, LANES), _F32)],
        grid=(t // rows,),
        in_specs=[pl.BlockSpec((1, rows), lambda i: (0, i)),
                  pl.BlockSpec(before.shape, lambda i: (0, 0)),
                  pl.BlockSpec(ones.shape, lambda i: (0, 0))],
        out_specs=[pl.BlockSpec((1, rows), lambda i: (0, i)),
                   pl.BlockSpec((LANES, LANES), lambda i: (0, 0))],
        scratch_shapes=[pltpu.VMEM((LANES, LANES), _F32)],
        compiler_params=pltpu.CompilerParams(dimension_semantics=("arbitrary",)),
        name="rank",
    )(cls, before, ones)


def _plan_kernel(cnt_ref, cls_ref, rank_ref,
                 ea_ref, eb_ref, grp_ref, nxt_ref, nv_ref, nu_ref, tok_ref, start_ref):
    i = pl.program_id(0)
    chunk = cls_ref.shape[0]
    n_slots = tok_ref.shape[0]
    n_blocks = nv_ref.shape[0]

    @pl.when(i == 0)
    def _():
        def pad(s, carry):
            tok_ref[s] = 0
            return carry

        blk = jnp.int32(0)
        for g in range(N_GROUPS):
            for p in range(N_PAIRS):
                c = g * N_PAIRS + p
                n = cnt_ref[c]
                nb = (n + (MOE_ROWS - 1)) // MOE_ROWS
                start = blk * MOE_ROWS
                start_ref[c] = start

                def fill(j, carry, blk=blk, n=n, g=g, p=p):
                    ea_ref[blk + j] = PAIR_A[p]
                    eb_ref[blk + j] = PAIR_B[p]
                    grp_ref[blk + j] = g
                    nv_ref[blk + j] = jnp.minimum(n - j * MOE_ROWS, MOE_ROWS)
                    return carry
                lax.fori_loop(0, nb, fill, 0)
                lax.fori_loop(start + n, start + nb * MOE_ROWS, pad, 0)
                blk = blk + nb
        nu_ref[0] = blk

        def idle(j, carry):
            ea_ref[j] = ea_ref[blk - 1]
            eb_ref[j] = eb_ref[blk - 1]
            grp_ref[j] = grp_ref[blk - 1]
            nxt_ref[j] = -1
            nv_ref[j] = 0
            return carry
        lax.fori_loop(blk, n_blocks, idle, 0)

        nxt_ref[blk - 1] = -1

        def following(k, carry):
            j = blk - 2 - k
            nxt_ref[j] = jnp.where(grp_ref[j + 1] != grp_ref[j], grp_ref[j + 1], nxt_ref[j + 1])
            return carry
        lax.fori_loop(0, blk - 1, following, 0)
        lax.fori_loop(blk * MOE_ROWS, n_slots, pad, 0)

    base = i * chunk

    def place(r, carry):
        tok_ref[start_ref[cls_ref[r]] + rank_ref[r]] = base + r
        return carry
    lax.fori_loop(0, chunk, place, 0, unroll=8)


def _plan(counts, cls, rank, n_blocks):
    t = cls.shape[0]
    chunk = min(PLAN_ROWS, t)
    smem = functools.partial(pl.BlockSpec, memory_space=pltpu.SMEM)
    whole = lambda n: smem((n,), lambda i: (0,))
    i32 = lambda n: jax.ShapeDtypeStruct((n,), jnp.int32)
    return pl.pallas_call(
        _plan_kernel,
        out_shape=[i32(n_blocks), i32(n_blocks), i32(n_blocks), i32(n_blocks), i32(n_blocks), i32(1),
                   i32(n_blocks * MOE_ROWS)],
        grid=(t // chunk,),
        in_specs=[whole(counts.shape[0]), smem((chunk,), lambda i: (i,)), smem((chunk,), lambda i: (i,))],
        out_specs=[whole(n_blocks), whole(n_blocks), whole(n_blocks), whole(n_blocks), whole(n_blocks),
                   whole(1), whole(n_blocks * MOE_ROWS)],
        scratch_shapes=[pltpu.SMEM((N_CLASSES,), jnp.int32)],
        compiler_params=pltpu.CompilerParams(dimension_semantics=("arbitrary",)),
        name="plan",
    )(counts, cls, rank)


def _moe_kernel(alpha, layer, ea_ref, eb_ref, grp_ref, nxt_ref, nv_ref, nu_ref, tok_ref, tokn_ref, hx_ref,
                wg_hbm, wu_hbm, wd_hbm, g_ref, b_ref, out_ref,
                xbuf, obuf, gsem, ssem, wg_f32, wu_f32, wd_f32, wg_ref, wu_ref, wd_ref, wsem):
    b = pl.program_id(0)
    nu = nu_ref[0]
    rows = MOE_ROWS

    stages = ((wg_hbm, wg_f32, wg_ref), (wu_hbm, wu_f32, wu_ref), (wd_hbm, wd_f32, wd_ref))

    def fetch_group_weights(group):
        for k, (src, stage, _) in enumerate(stages):
            pltpu.make_async_copy(src.at[layer * N_GROUPS + group], stage, wsem.at[k]).start(priority=1)

    def adopt_group_weights():
        for k, (src, stage, dst) in enumerate(stages):
            pltpu.make_async_copy(src.at[0], stage, wsem.at[k]).wait()
            for e in range(EXPERTS_PER_GROUP):
                dst[e] = stage[e].astype(_BF16)

    def gather_start(tokens, s):
        for r in range(rows):
            pltpu.make_async_copy(hx_ref.at[tokens[r]], xbuf.at[s, r], gsem.at[s]).start()

    def gather_wait(s):
        pltpu.make_async_copy(hx_ref.at[pl.ds(0, rows)], xbuf.at[s], gsem.at[s]).wait()

    def scatter_wait(s, n):
        @pl.when(n > 0)
        def _():
            pltpu.make_async_copy(obuf.at[s, pl.ds(0, n)], out_ref.at[pl.ds(0, n)], ssem.at[s]).wait()

    def step(s):
        o = 1 - s

        @pl.when(b == 0)
        def _():
            gather_start(tok_ref, s)
            fetch_group_weights(grp_ref[b])

        @pl.when(jnp.logical_or(b == 0, grp_ref[b] != grp_ref[jnp.maximum(b - 1, 0)]))
        def _():
            adopt_group_weights()

            @pl.when(nxt_ref[b] >= 0)
            def _():
                fetch_group_weights(nxt_ref[b])
        gather_wait(s)
        gather_start(tokn_ref, o)

        @pl.when(b >= 2)
        def _():
            scatter_wait(s, nv_ref[jnp.maximum(b - 2, 0)])

        xe = _load_rows(xbuf.at[s], ROW_TILES + SUBLANES)
        x = xe[:, 0:D_MODEL]
        xb = x.astype(_BF16)
        wts = xe[:, D_MODEL:D_MODEL + LANES]
        y = None
        for lane, e_ref in enumerate((ea_ref, eb_ref)):
            e = e_ref[b]
            a = _dot(xb, wg_ref[e])
            u = _dot(xb, wu_ref[e])
            hid = a / (1.0 + jnp.exp(-a)) * u * wts[:, lane:lane + 1]
            part = _dot(hid.astype(_BF16), wd_ref[e])
            y = part if y is None else y + part
        _store_rows(obuf.at[s], _layer_norm(alpha * x + y, g_ref[...], b_ref[...]))

        n_valid = nv_ref[b]

        @pl.when(n_valid == rows)
        def _():
            for r in range(rows):
                pltpu.make_async_copy(obuf.at[s, r], out_ref.at[tok_ref[r]], ssem.at[s]).start()

        @pl.when(n_valid < rows)
        def _():
            for r in range(rows):
                @pl.when(r < n_valid)
                def _():
                    pltpu.make_async_copy(obuf.at[s, r], out_ref.at[tok_ref[r]], ssem.at[s]).start()

        @pl.when(b == nu - 1)
        def _():
            gather_wait(o)

            @pl.when(b >= 1)
            def _():
                scatter_wait(o, nv_ref[jnp.maximum(b - 1, 0)])
            scatter_wait(s, n_valid)

    for s in range(2):
        pl.when(jnp.logical_and(b < nu, b % 2 == s))(functools.partial(step, s))


def _moe(alpha, layer, ea, eb, grp, nxt, nv, nu, tok, hx, wg, wu, wd, ln_g, ln_b):
    t = hx.shape[0]
    n_blocks = nv.shape[0]
    rows = MOE_ROWS
    last = n_blocks - 1
    full = lambda a: pl.BlockSpec(a.shape, lambda i, ea, eb, grp, nxt, nv, nu: (0,) * a.ndim)
    grid_spec = pltpu.PrefetchScalarGridSpec(
        num_scalar_prefetch=6,
        grid=(n_blocks,),
        in_specs=[pl.BlockSpec((rows,), lambda i, ea, eb, grp, nxt, nv, nu: (i,), memory_space=pltpu.SMEM),
                  pl.BlockSpec((rows,), lambda i, ea, eb, grp, nxt, nv, nu: (jnp.minimum(i + 1, last),),
                               memory_space=pltpu.SMEM),
                  pl.BlockSpec(memory_space=pl.ANY),
                  pl.BlockSpec(memory_space=pl.ANY), pl.BlockSpec(memory_space=pl.ANY),
                  pl.BlockSpec(memory_space=pl.ANY), full(ln_g), full(ln_b)],
        out_specs=pl.BlockSpec(memory_space=pl.ANY),
        scratch_shapes=[pltpu.VMEM((2, rows, EXT_TILES, LANES), _F32),
                        pltpu.VMEM((2, rows, ROW_TILES, LANES), _F32),
                        pltpu.SemaphoreType.DMA((2,)),
                        pltpu.SemaphoreType.DMA((2,)),
                        pltpu.VMEM(wg.shape[1:], _F32), pltpu.VMEM(wu.shape[1:], _F32),
                        pltpu.VMEM(wd.shape[1:], _F32),
                        pltpu.VMEM(wg.shape[1:], _BF16), pltpu.VMEM(wu.shape[1:], _BF16),
                        pltpu.VMEM(wd.shape[1:], _BF16),
                        pltpu.SemaphoreType.DMA((3,))],
    )
    return pl.pallas_call(
        functools.partial(_moe_kernel, alpha, layer),
        out_shape=jax.ShapeDtypeStruct((t, ROW_TILES, LANES), _F32),
        grid_spec=grid_spec,
        compiler_params=pltpu.CompilerParams(
            dimension_semantics=("arbitrary",), vmem_limit_bytes=VMEM_LIMIT),
        name="moe",
    )(ea, eb, grp, nxt, nv, nu, tok, tok, hx, wg, wu, wd, ln_g, ln_b)


def kernel(x, ln0_g, ln0_b, w_in, conv_w, g_conv, g_attn, w_out, ln_mix_g, ln_mix_b,
           router_w, router_b, w_gate, w_up, w_down, ln_ffn_g, ln_ffn_b):
    bsz, seq, d = x.shape
    depth = w_in.shape[0]
    t = bsz * seq
    alpha = (2.0 * depth) ** 0.25
    row2 = lambda v: v.reshape(1, -1)

    order = jnp.asarray([(r % N_GROUPS) * EXPERTS_PER_GROUP + r // N_GROUPS for r in range(N_EXPERTS)])
    rw = router_w.T[order]
    rw_hi = rw.astype(_BF16)
    rw_lo = (rw - rw_hi.astype(_F32)).astype(_BF16)
    rb = router_b[order].reshape(N_EXPERTS, 1)

    kidx = jnp.arange(ATTN_ROWS)
    ux = (kidx[:, None] > kidx[None, :]).astype(_BF16)

    by_group = lambda w: w.reshape((depth * N_GROUPS, EXPERTS_PER_GROUP) + w.shape[2:])

    n_slots = (t + N_CLASSES * (MOE_ROWS - 1) + MOE_ROWS - 1) // MOE_ROWS * MOE_ROWS
    n_blocks = n_slots // MOE_ROWS

    h = x.reshape(t, d)
    for l in range(depth):
        w_in_bf = w_in[l].astype(_BF16)
        if l == 0:
            h, mc, qkv = _proj_conv(h, (row2(ln0_g), row2(ln0_b)), w_in_bf, conv_w[l],
                                    row2(g_conv[l]), seq)
        else:
            mc, qkv = _proj_conv(h, None, w_in_bf, conv_w[l], row2(g_conv[l]), seq)
        ma = _attention(qkv.reshape(bsz, seq, 3 * D_ATTN), ux, row2(g_attn[l])).reshape(t, D_ATTN)
        hx, cls = _mix_route(alpha, mc, ma, h, w_out[l].astype(_BF16), row2(ln_mix_g[l]),
                             row2(ln_mix_b[l]), rw_hi, rw_lo, rb)
        rank, cnt = _rank(cls)
        counts = cnt[:, 0].astype(jnp.int32)
        ea, eb, grp, nxt, nv, nu, tok = _plan(counts, cls.reshape(t), rank.reshape(t), n_blocks)
        h = _moe(alpha, l, ea, eb, grp, nxt, nv, nu, tok, hx, by_group(w_gate), by_group(w_up),
                 by_group(w_down), row2(ln_ffn_g[l]), row2(ln_ffn_b[l]))
    return h.reshape(bsz, seq, d)
```

```python
import functools
import math

import jax
import jax.numpy as jnp
from jax import lax
from jax.experimental import pallas as pl
from jax.experimental.pallas import tpu as pltpu

D_MODEL = 1024
D_CONV = 512
D_ATTN = 512
HEAD_DIM = 64
N_HEADS = D_ATTN // HEAD_DIM
CONV_K = 3
N_EXPERTS = 32
N_GROUPS = 8
EXPERTS_PER_GROUP = 4
D_EXPERT = 512
LN_EPS = 1e-5
RMS_EPS = 1e-6

PAIR_A = (0, 0, 0, 1, 1, 2)
PAIR_B = (1, 2, 3, 3, 2, 3)
N_PAIRS = len(PAIR_A)
N_CLASSES = N_GROUPS * N_PAIRS

LANES = 128
SUBLANES = 8
ROW_TILES = D_MODEL // LANES
EXT_TILES = 2 * ROW_TILES

PROJ_ROWS = 1024
ATTN_ROWS = 256
MIX_ROWS = 1024
RANK_ROWS = 512
MOE_ROWS = 128
PLAN_ROWS = 2048

ATTN_STOP = -88.0

VMEM_LIMIT = 56 * 1024 * 1024

_BF16 = jnp.bfloat16
_F32 = jnp.float32


def _dot(a, b):
    return jnp.dot(a, b, preferred_element_type=_F32)


def _dot_nt(a, b):
    return lax.dot_general(a, b, (((1,), (1,)), ((), ())), preferred_element_type=_F32)


def _layer_norm(x, g, b):
    mu = jnp.mean(x, axis=-1, keepdims=True)
    xc = x - mu
    var = jnp.mean(xc * xc, axis=-1, keepdims=True)
    return xc * lax.rsqrt(var + LN_EPS) * g + b


def _rms_norm(x, g):
    ms = jnp.mean(x * x, axis=-1, keepdims=True)
    return x * lax.rsqrt(ms + RMS_EPS) * g


def _load_rows(ref, n_tiles=ROW_TILES):
    rows = ref.shape[0]
    return ref[:, 0:n_tiles, :].reshape(rows, n_tiles * LANES)


def _store_rows(ref, val):
    rows, cols = val.shape
    ref[:, 0:cols // LANES, :] = val.reshape(rows, cols // LANES, LANES)


def _proj_kernel(apply_ln, tiles_per_seq, *refs):
    if apply_ln:
        x_ref, g0_ref, b0_ref, w_ref, cw_ref, gc_ref, h_ref, mc_ref, qkv_ref, carry_ref = refs
    else:
        x_ref, w_ref, cw_ref, gc_ref, mc_ref, qkv_ref, carry_ref = refs
    i = pl.program_id(0)
    rows = x_ref.shape[0]

    if apply_ln:
        h = _layer_norm(x_ref[...], g0_ref[...], b0_ref[...])
        _store_rows(h_ref, h)
    else:
        h = _load_rows(x_ref)
    hb = h.astype(_BF16)

    cb = _dot(hb, w_ref[:, 0:D_CONV])
    cc = _dot(hb, w_ref[:, D_CONV:2 * D_CONV])
    cx = _dot(hb, w_ref[:, 2 * D_CONV:3 * D_CONV])
    u = cc * cx

    @pl.when(i % tiles_per_seq == 0)
    def _():
        carry_ref[...] = jnp.zeros_like(carry_ref)

    prev = carry_ref[...]
    p1 = prev[SUBLANES - 1:SUBLANES, :]
    p2 = prev[SUBLANES - 2:SUBLANES - 1, :]
    row = lax.broadcasted_iota(jnp.int32, (rows, 1), 0)
    u1 = jnp.where(row == 0, p1, pltpu.roll(u, 1, 0))
    u2 = jnp.where(row == 0, p2, jnp.where(row == 1, p1, pltpu.roll(u, 2, 0)))
    carry_ref[...] = u[rows - SUBLANES:rows, :]

    cw = cw_ref[...]
    conv = cb * (cw[0:1, :] * u2 + cw[1:2, :] * u1 + cw[2:3, :] * u)
    mc_ref[...] = _rms_norm(conv, gc_ref[...]).astype(_BF16)

    qkv = _dot(hb, w_ref[:, 3 * D_CONV:])
    qkv_ref[:, 0:D_ATTN] = (qkv[:, 0:D_ATTN] * (1.0 / math.sqrt(HEAD_DIM))).astype(_BF16)
    qkv_ref[:, D_ATTN:] = qkv[:, D_ATTN:].astype(_BF16)


def _proj_conv(x, ln0, w_in_bf, conv_w, g_conv, seq):
    t = x.shape[0]
    rows = PROJ_ROWS
    apply_ln = ln0 is not None
    in_cols = w_in_bf.shape[1]
    row_spec = lambda c: pl.BlockSpec((rows, c), lambda i: (i, 0))
    tile_spec = pl.BlockSpec((rows, ROW_TILES, LANES), lambda i: (i, 0, 0))
    full = lambda a: pl.BlockSpec(a.shape, lambda i: (0,) * a.ndim)
    args, in_specs = [x], [row_spec(D_MODEL) if apply_ln else tile_spec]
    if apply_ln:
        args += [ln0[0], ln0[1]]
        in_specs += [full(ln0[0]), full(ln0[1])]
    args += [w_in_bf, conv_w, g_conv]
    in_specs += [full(w_in_bf), full(conv_w), full(g_conv)]
    out_shape, out_specs = [], []
    if apply_ln:
        out_shape.append(jax.ShapeDtypeStruct((t, ROW_TILES, LANES), _F32))
        out_specs.append(tile_spec)
    out_shape += [jax.ShapeDtypeStruct((t, D_CONV), _BF16),
                  jax.ShapeDtypeStruct((t, in_cols - 3 * D_CONV), _BF16)]
    out_specs += [row_spec(D_CONV), row_spec(in_cols - 3 * D_CONV)]
    return pl.pallas_call(
        functools.partial(_proj_kernel, apply_ln, seq // rows),
        out_shape=out_shape,
        grid=(t // rows,),
        in_specs=in_specs,
        out_specs=out_specs,
        scratch_shapes=[pltpu.VMEM((SUBLANES, D_CONV), _F32)],
        compiler_params=pltpu.CompilerParams(
            dimension_semantics=("arbitrary",), vmem_limit_bytes=VMEM_LIMIT),
        name="proj_conv",
    )(*args)


def _attn_kernel(q_ref, k_ref, v_ref, ux_ref, ga_ref, o_ref, acc_ref, r_ref, lk_ref, ls_ref, sx_ref):
    i = pl.program_id(1)
    tq = q_ref.shape[0]
    tk = tq

    qpos = lax.broadcasted_iota(jnp.int32, (tq, tk), 0)
    kpos = lax.broadcasted_iota(jnp.int32, (tq, tk), 1)
    causal = kpos < qpos

    def block(j, diagonal):
        ks = pl.multiple_of(j * tk, tk)
        for hd in range(N_HEADS):
            cols = slice(hd * HEAD_DIM, (hd + 1) * HEAD_DIM)
            z = _dot_nt(q_ref[:, cols], k_ref[pl.ds(ks, tk), cols])
            zb = z.astype(_BF16)
            nsp = jnp.minimum(-zb, 0.0) - jnp.log(1.0 + jnp.exp(-jnp.abs(zb)))
            ls_ref[hd] = z + nsp.astype(_F32)
            lk_ref[hd] = jnp.where(causal, nsp, jnp.zeros_like(nsp)) if diagonal else nsp
        sx_ref[...] = _dot(lk_ref[...].reshape(N_HEADS * tq, tk), ux_ref[...])
        for hd in range(N_HEADS):
            cols = slice(hd * HEAD_DIM, (hd + 1) * HEAD_DIM)
            rows = slice(hd * tq, (hd + 1) * tq)
            total = jnp.broadcast_to(
                sx_ref[rows, 0:1] + lk_ref[hd, :, 0:1].astype(_F32), (tq, LANES))
            if diagonal:
                a = jnp.where(causal, jnp.exp(ls_ref[hd] + sx_ref[rows, :]), 0.0)
                acc_ref[:, cols] = _dot(a.astype(_BF16), v_ref[pl.ds(ks, tk), cols])
                r_ref[hd] = total
            else:
                r = r_ref[hd]
                arg = ls_ref[hd] + sx_ref[rows, :] + jnp.concatenate([r] * (tk // LANES), axis=1)
                acc_ref[:, cols] += _dot(jnp.exp(arg).astype(_BF16), v_ref[pl.ds(ks, tk), cols])
                r_ref[hd] = r + total

    block(i, True)

    def cond(j):
        return jnp.logical_and(j >= 0, jnp.max(r_ref[...]) > ATTN_STOP)

    def body(j):
        block(j, False)
        return j - 1

    lax.while_loop(cond, body, i - 1)

    o_ref[...] = _rms_norm(acc_ref[...], ga_ref[...]).astype(_BF16)


def _attention(qkv3, ux, g_attn):
    b, s, _ = qkv3.shape
    tq = ATTN_ROWS
    return pl.pallas_call(
        _attn_kernel,
        out_shape=jax.ShapeDtypeStruct((b, s, D_ATTN), _BF16),
        grid=(b, s // tq),
        in_specs=[
            pl.BlockSpec((None, tq, D_ATTN), lambda bi, i: (bi, i, 0)),
            pl.BlockSpec((None, s, D_ATTN), lambda bi, i: (bi, 0, 1)),
            pl.BlockSpec((None, s, D_ATTN), lambda bi, i: (bi, 0, 2)),
            pl.BlockSpec(ux.shape, lambda bi, i: (0, 0)),
            pl.BlockSpec(g_attn.shape, lambda bi, i: (0, 0)),
        ],
        out_specs=pl.BlockSpec((None, tq, D_ATTN), lambda bi, i: (bi, i, 0)),
        scratch_shapes=[pltpu.VMEM((tq, D_ATTN), _F32),
                        pltpu.VMEM((N_HEADS, tq, LANES), _F32),
                        pltpu.VMEM((N_HEADS, tq, tq), _BF16),
                        pltpu.VMEM((N_HEADS, tq, tq), _F32),
                        pltpu.VMEM((N_HEADS * tq, tq), _F32)],
        compiler_params=pltpu.CompilerParams(
            dimension_semantics=("arbitrary", "arbitrary"), vmem_limit_bytes=VMEM_LIMIT),
        name="attention",
    )(qkv3, qkv3, qkv3, ux, g_attn)


def _mix_kernel(alpha, mc_ref, ma_ref, h_ref, wo_ref, g_ref, b_ref, rwh_ref, rwl_ref, rb_ref,
                hx_ref, cls_ref):
    rows = h_ref.shape[0]
    mix = _dot(mc_ref[...], wo_ref[0:D_CONV, :]) + _dot(ma_ref[...], wo_ref[D_CONV:, :])
    h1 = _layer_norm(alpha * _load_rows(h_ref) + mix, g_ref[...], b_ref[...])

    hi = h1.astype(_BF16)
    lo = (h1 - hi.astype(_F32)).astype(_BF16)
    logits = (_dot_nt(rwh_ref[...], hi) + _dot_nt(rwh_ref[...], lo) + _dot_nt(rwl_ref[...], hi))
    score = 1.0 / (1.0 + jnp.exp(-logits))
    sel = score + rb_ref[...]
    s = [sel[j * N_GROUPS:(j + 1) * N_GROUPS, :] for j in range(EXPERTS_PER_GROUP)]
    sc = [score[j * N_GROUPS:(j + 1) * N_GROUPS, :] for j in range(EXPERTS_PER_GROUP)]

    gs = None
    for a in range(EXPERTS_PER_GROUP):
        for c in range(a + 1, EXPERTS_PER_GROUP):
            pair = s[a] + s[c]
            gs = pair if gs is None else jnp.maximum(gs, pair)
    gidx = lax.broadcasted_iota(jnp.int32, gs.shape, 0)
    gmax = jnp.max(gs, axis=0, keepdims=True)
    g = jnp.min(jnp.where(gs == gmax, gidx, N_GROUPS), axis=0, keepdims=True)
    in_g = gidx == g
    sv = [jnp.sum(jnp.where(in_g, s[j], 0.0), axis=0, keepdims=True) for j in range(EXPERTS_PER_GROUP)]
    cv = [jnp.sum(jnp.where(in_g, sc[j], 0.0), axis=0, keepdims=True) for j in range(EXPERTS_PER_GROUP)]

    chosen = []
    for j in range(EXPERTS_PER_GROUP):
        rank = jnp.zeros(sv[j].shape, jnp.int32)
        for c in range(EXPERTS_PER_GROUP):
            if c == j:
                continue
            ahead = (sv[c] >= sv[j]) if c < j else (sv[c] > sv[j])
            rank = rank + ahead.astype(jnp.int32)
        chosen.append(rank < 2)
    den = sum(jnp.where(chosen[j], cv[j], 0.0) for j in range(EXPERTS_PER_GROUP))
    wgt = [jnp.where(chosen[j], cv[j] / den, 0.0) for j in range(EXPERTS_PER_GROUP)]
    first = jnp.full(den.shape, EXPERTS_PER_GROUP, jnp.int32)
    last = jnp.full(den.shape, -1, jnp.int32)
    for j in range(EXPERTS_PER_GROUP):
        first = jnp.where(chosen[j], jnp.minimum(first, j), first)
        last = jnp.where(chosen[j], jnp.maximum(last, j), last)
    pair_id = jnp.zeros(den.shape, jnp.int32)
    for p in range(N_PAIRS):
        pair_id = jnp.where(jnp.logical_and(first == PAIR_A[p], last == PAIR_B[p]), p, pair_id)
    w_first = sum(jnp.where(first == j, wgt[j], 0.0) for j in range(EXPERTS_PER_GROUP))
    w_last = sum(jnp.where(last == j, wgt[j], 0.0) for j in range(EXPERTS_PER_GROUP))
    cls_ref[...] = g * N_PAIRS + pair_id

    sub = lax.broadcasted_iota(jnp.int32, (LANES, rows), 0)
    slab = jnp.where(sub == 0, w_first, jnp.where(sub == 1, w_last, 0.0))
    routed = jnp.concatenate(
        [h1, slab.T, jnp.zeros((rows, (EXT_TILES - ROW_TILES - 1) * LANES), _F32)], axis=1)
    _store_rows(hx_ref, routed)


def _mix_route(alpha, mc, ma, h, w_out_bf, ln_g, ln_b, rw_hi, rw_lo, rb):
    t = h.shape[0]
    rows = MIX_ROWS
    row_spec = lambda c: pl.BlockSpec((rows, c), lambda i: (i, 0))
    full = lambda a: pl.BlockSpec(a.shape, lambda i: (0,) * a.ndim)
    return pl.pallas_call(
        functools.partial(_mix_kernel, alpha),
        out_shape=[jax.ShapeDtypeStruct((t, EXT_TILES, LANES), _F32),
                   jax.ShapeDtypeStruct((1, t), jnp.int32)],
        grid=(t // rows,),
        in_specs=[row_spec(D_CONV), row_spec(D_ATTN),
                  pl.BlockSpec((rows, ROW_TILES, LANES), lambda i: (i, 0, 0)), full(w_out_bf),
                  full(ln_g), full(ln_b), full(rw_hi), full(rw_lo), full(rb)],
        out_specs=[pl.BlockSpec((rows, EXT_TILES, LANES), lambda i: (i, 0, 0)),
                   pl.BlockSpec((1, rows), lambda i: (0, i))],
        compiler_params=pltpu.CompilerParams(
            dimension_semantics=("arbitrary",), vmem_limit_bytes=VMEM_LIMIT),
        name="mix_route",
    )(mc, ma, h, w_out_bf, ln_g, ln_b, rw_hi, rw_lo, rb)


def _rank_kernel(cls_ref, before_ref, ones_ref, rank_ref, cnt_ref, run_ref):
    i = pl.program_id(0)
    rows = cls_ref.shape[1]

    @pl.when(i == 0)
    def _():
        run_ref[...] = jnp.zeros_like(run_ref)

    cidx = lax.broadcasted_iota(jnp.int32, (LANES, rows), 0)
    onehot = cidx == cls_ref[...]
    oh = jnp.where(onehot, 1.0, 0.0).astype(_BF16)
    within = _dot(oh, before_ref[...])
    run = run_ref[...]
    total = within + jnp.concatenate([run] * (rows // LANES), axis=1)
    rank_ref[...] = jnp.sum(jnp.where(onehot, total, 0.0), axis=0, keepdims=True).astype(jnp.int32)
    run = run + _dot(oh, ones_ref[...])
    run_ref[...] = run
    cnt_ref[...] = run


def _rank(cls):
    t = cls.shape[1]
    rows = RANK_ROWS
    idx = jnp.arange(rows)
    before = (idx[:, None] < idx[None, :]).astype(_BF16)
    ones = jnp.ones((rows, LANES), _BF16)
    return pl.pallas_call(
        _rank_kernel,
        out_shape=[jax.ShapeDtypeStruct((1, t), jnp.int32),
                   jax.ShapeDtypeStruct((LANES, LANES), _F32)],
        grid=(t // rows,),
        in_specs=[pl.BlockSpec((1, rows), lambda i: (0, i)),
                  pl.BlockSpec(before.shape, lambda i: (0, 0)),
                  pl.BlockSpec(ones.shape, lambda i: (0, 0))],
        out_specs=[pl.BlockSpec((1, rows), lambda i: (0, i)),
                   pl.BlockSpec((LANES, LANES), lambda i: (0, 0))],
        scratch_shapes=[pltpu.VMEM((LANES, LANES), _F32)],
        compiler_params=pltpu.CompilerParams(dimension_semantics=("arbitrary",)),
        name="rank",
    )(cls, before, ones)


def _plan_kernel(cnt_ref, cls_ref, rank_ref,
                 ea_ref, eb_ref, grp_ref, nxt_ref, nv_ref, nu_ref, tok_ref, start_ref):
    i = pl.program_id(0)
    chunk = cls_ref.shape[0]
    n_slots = tok_ref.shape[0]
    n_blocks = nv_ref.shape[0]

    @pl.when(i == 0)
    def _():
        def pad(s, carry):
            tok_ref[s] = 0
            return carry

        blk = jnp.int32(0)
        for g in range(N_GROUPS):
            for p in range(N_PAIRS):
                c = g * N_PAIRS + p
                n = cnt_ref[c]
                nb = (n + (MOE_ROWS - 1)) // MOE_ROWS
                start = blk * MOE_ROWS
                start_ref[c] = start

                def fill(j, carry, blk=blk, n=n, g=g, p=p):
                    ea_ref[blk + j] = PAIR_A[p]
                    eb_ref[blk + j] = PAIR_B[p]
                    grp_ref[blk + j] = g
                    nv_ref[blk + j] = jnp.minimum(n - j * MOE_ROWS, MOE_ROWS)
                    return carry
                lax.fori_loop(0, nb, fill, 0)
                lax.fori_loop(start + n, start + nb * MOE_ROWS, pad, 0)
                blk = blk + nb
        nu_ref[0] = blk

        def idle(j, carry):
            ea_ref[j] = ea_ref[blk - 1]
            eb_ref[j] = eb_ref[blk - 1]
            grp_ref[j] = grp_ref[blk - 1]
            nxt_ref[j] = -1
            nv_ref[j] = 0
            return carry
        lax.fori_loop(blk, n_blocks, idle, 0)

        nxt_ref[blk - 1] = -1

        def following(k, carry):
            j = blk - 2 - k
            nxt_ref[j] = jnp.where(grp_ref[j + 1] != grp_ref[j], grp_ref[j + 1], nxt_ref[j + 1])
            return carry
        lax.fori_loop(0, blk - 1, following, 0)
        lax.fori_loop(blk * MOE_ROWS, n_slots, pad, 0)

    base = i * chunk

    def place(r, carry):
        tok_ref[start_ref[cls_ref[r]] + rank_ref[r]] = base + r
        return carry
    lax.fori_loop(0, chunk, place, 0, unroll=8)


def _plan(counts, cls, rank, n_blocks):
    t = cls.shape[0]
    chunk = min(PLAN_ROWS, t)
    smem = functools.partial(pl.BlockSpec, memory_space=pltpu.SMEM)
    whole = lambda n: smem((n,), lambda i: (0,))
    i32 = lambda n: jax.ShapeDtypeStruct((n,), jnp.int32)
    return pl.pallas_call(
        _plan_kernel,
        out_shape=[i32(n_blocks), i32(n_blocks), i32(n_blocks), i32(n_blocks), i32(n_blocks), i32(1),
                   i32(n_blocks * MOE_ROWS)],
        grid=(t // chunk,),
        in_specs=[whole(counts.shape[0]), smem((chunk,), lambda i: (i,)), smem((chunk,), lambda i: (i,))],
        out_specs=[whole(n_blocks), whole(n_blocks), whole(n_blocks), whole(n_blocks), whole(n_blocks),
                   whole(1), whole(n_blocks * MOE_ROWS)],
        scratch_shapes=[pltpu.SMEM((N_CLASSES,), jnp.int32)],
        compiler_params=pltpu.CompilerParams(dimension_semantics=("arbitrary",)),
        name="plan",
    )(counts, cls, rank)


def _moe_kernel(alpha, layer, ea_ref, eb_ref, grp_ref, nxt_ref, nv_ref, nu_ref, tok_ref, tokn_ref, tokn2_ref, hx_ref,
                wg_hbm, wu_hbm, wd_hbm, g_ref, b_ref, out_ref,
                xbuf, obuf, gsem, ssem, wg_f32, wu_f32, wd_f32, wg_ref, wu_ref, wd_ref, wsem):
    b = pl.program_id(0)
    nu = nu_ref[0]
    rows = MOE_ROWS

    stages = ((wg_hbm, wg_f32, wg_ref), (wu_hbm, wu_f32, wu_ref), (wd_hbm, wd_f32, wd_ref))

    def fetch_group_weights(group):
        for k, (src, stage, _) in enumerate(stages):
            pltpu.make_async_copy(src.at[layer * N_GROUPS + group], stage, wsem.at[k]).start(priority=1)

    def adopt_group_weights():
        for k, (src, stage, dst) in enumerate(stages):
            pltpu.make_async_copy(src.at[0], stage, wsem.at[k]).wait()
            for e in range(EXPERTS_PER_GROUP):
                dst[e] = stage[e].astype(_BF16)

    def gather_start(tokens, s):
        for r in range(rows):
            pltpu.make_async_copy(hx_ref.at[tokens[r]], xbuf.at[s, r], gsem.at[s]).start()

    def gather_wait(s):
        pltpu.make_async_copy(hx_ref.at[pl.ds(0, rows)], xbuf.at[s], gsem.at[s]).wait()

    def scatter_wait(s, n):
        @pl.when(n > 0)
        def _():
            pltpu.make_async_copy(obuf.at[s, pl.ds(0, n)], out_ref.at[pl.ds(0, n)], ssem.at[s]).wait()

    def step(s, gs):
        o = 1 - s
        gn = (gs + 2) % 3

        @pl.when(b == 0)
        def _():
            gather_start(tok_ref, gs)
            gather_start(tokn_ref, (gs + 1) % 3)
            fetch_group_weights(grp_ref[b])

        @pl.when(jnp.logical_or(b == 0, grp_ref[b] != grp_ref[jnp.maximum(b - 1, 0)]))
        def _():
            adopt_group_weights()

            @pl.when(nxt_ref[b] >= 0)
            def _():
                fetch_group_weights(nxt_ref[b])
        gather_wait(gs)
        gather_start(tokn2_ref, gn)

        @pl.when(b >= 2)
        def _():
            scatter_wait(s, nv_ref[jnp.maximum(b - 2, 0)])

        xe = _load_rows(xbuf.at[gs], ROW_TILES + SUBLANES)
        x = xe[:, 0:D_MODEL]
        xb = x.astype(_BF16)
        wts = xe[:, D_MODEL:D_MODEL + LANES]
        y = None
        for lane, e_ref in enumerate((ea_ref, eb_ref)):
            e = e_ref[b]
            a = _dot(xb, wg_ref[e])
            u = _dot(xb, wu_ref[e])
            hid = a / (1.0 + jnp.exp(-a)) * u * wts[:, lane:lane + 1]
            part = _dot(hid.astype(_BF16), wd_ref[e])
            y = part if y is None else y + part
        _store_rows(obuf.at[s], _layer_norm(alpha * x + y, g_ref[...], b_ref[...]))

        n_valid = nv_ref[b]

        @pl.when(n_valid == rows)
        def _():
            for r in range(rows):
                pltpu.make_async_copy(obuf.at[s, r], out_ref.at[tok_ref[r]], ssem.at[s]).start()

        @pl.when(n_valid < rows)
        def _():
            for r in range(rows):
                @pl.when(r < n_valid)
                def _():
                    pltpu.make_async_copy(obuf.at[s, r], out_ref.at[tok_ref[r]], ssem.at[s]).start()

        @pl.when(b == nu - 1)
        def _():
            gather_wait((gs + 1) % 3)
            gather_wait(gn)

            @pl.when(b >= 1)
            def _():
                scatter_wait(o, nv_ref[jnp.maximum(b - 1, 0)])
            scatter_wait(s, n_valid)

    for k in range(6):
        pl.when(jnp.logical_and(b < nu, b % 6 == k))(functools.partial(step, k % 2, k % 3))


def _moe(alpha, layer, ea, eb, grp, nxt, nv, nu, tok, hx, wg, wu, wd, ln_g, ln_b):
    t = hx.shape[0]
    n_blocks = nv.shape[0]
    rows = MOE_ROWS
    last = n_blocks - 1
    full = lambda a: pl.BlockSpec(a.shape, lambda i, ea, eb, grp, nxt, nv, nu: (0,) * a.ndim)
    grid_spec = pltpu.PrefetchScalarGridSpec(
        num_scalar_prefetch=6,
        grid=(n_blocks,),
        in_specs=[pl.BlockSpec((rows,), lambda i, ea, eb, grp, nxt, nv, nu: (i,), memory_space=pltpu.SMEM),
                  pl.BlockSpec((rows,), lambda i, ea, eb, grp, nxt, nv, nu: (jnp.minimum(i + 1, last),),
                               memory_space=pltpu.SMEM),
                  pl.BlockSpec((rows,), lambda i, ea, eb, grp, nxt, nv, nu: (jnp.minimum(i + 2, last),),
                               memory_space=pltpu.SMEM),
                  pl.BlockSpec(memory_space=pl.ANY),
                  pl.BlockSpec(memory_space=pl.ANY), pl.BlockSpec(memory_space=pl.ANY),
                  pl.BlockSpec(memory_space=pl.ANY), full(ln_g), full(ln_b)],
        out_specs=pl.BlockSpec(memory_space=pl.ANY),
        scratch_shapes=[pltpu.VMEM((3, rows, EXT_TILES, LANES), _F32),
                        pltpu.VMEM((2, rows, ROW_TILES, LANES), _F32),
                        pltpu.SemaphoreType.DMA((3,)),
                        pltpu.SemaphoreType.DMA((2,)),
                        pltpu.VMEM(wg.shape[1:], _F32), pltpu.VMEM(wu.shape[1:], _F32),
                        pltpu.VMEM(wd.shape[1:], _F32),
                        pltpu.VMEM(wg.shape[1:], _BF16), pltpu.VMEM(wu.shape[1:], _BF16),
                        pltpu.VMEM(wd.shape[1:], _BF16),
                        pltpu.SemaphoreType.DMA((3,))],
    )
    return pl.pallas_call(
        functools.partial(_moe_kernel, alpha, layer),
        out_shape=jax.ShapeDtypeStruct((t, ROW_TILES, LANES), _F32),
        grid_spec=grid_spec,
        compiler_params=pltpu.CompilerParams(
            dimension_semantics=("arbitrary",), vmem_limit_bytes=VMEM_LIMIT),
        name="moe",
    )(ea, eb, grp, nxt, nv, nu, tok, tok, tok, hx, wg, wu, wd, ln_g, ln_b)


def kernel(x, ln0_g, ln0_b, w_in, conv_w, g_conv, g_attn, w_out, ln_mix_g, ln_mix_b,
           router_w, router_b, w_gate, w_up, w_down, ln_ffn_g, ln_ffn_b):
    bsz, seq, d = x.shape
    depth = w_in.shape[0]
    t = bsz * seq
    alpha = (2.0 * depth) ** 0.25
    row2 = lambda v: v.reshape(1, -1)

    order = jnp.asarray([(r % N_GROUPS) * EXPERTS_PER_GROUP + r // N_GROUPS for r in range(N_EXPERTS)])
    rw = router_w.T[order]
    rw_hi = rw.astype(_BF16)
    rw_lo = (rw - rw_hi.astype(_F32)).astype(_BF16)
    rb = router_b[order].reshape(N_EXPERTS, 1)

    kidx = jnp.arange(ATTN_ROWS)
    ux = (kidx[:, None] > kidx[None, :]).astype(_BF16)

    by_group = lambda w: w.reshape((depth * N_GROUPS, EXPERTS_PER_GROUP) + w.shape[2:])

    n_slots = (t + N_CLASSES * (MOE_ROWS - 1) + MOE_ROWS - 1) // MOE_ROWS * MOE_ROWS
    n_blocks = n_slots // MOE_ROWS

    h = x.reshape(t, d)
    for l in range(depth):
        w_in_bf = w_in[l].astype(_BF16)
        if l == 0:
            h, mc, qkv = _proj_conv(h, (row2(ln0_g), row2(ln0_b)), w_in_bf, conv_w[l],
                                    row2(g_conv[l]), seq)
        else:
            mc, qkv = _proj_conv(h, None, w_in_bf, conv_w[l], row2(g_conv[l]), seq)
        ma = _attention(qkv.reshape(bsz, seq, 3 * D_ATTN), ux, row2(g_attn[l])).reshape(t, D_ATTN)
        hx, cls = _mix_route(alpha, mc, ma, h, w_out[l].astype(_BF16), row2(ln_mix_g[l]),
                             row2(ln_mix_b[l]), rw_hi, rw_lo, rb)
        rank, cnt = _rank(cls)
        counts = cnt[:, 0].astype(jnp.int32)
        ea, eb, grp, nxt, nv, nu, tok = _plan(counts, cls.reshape(t), rank.reshape(t), n_blocks)
        h = _moe(alpha, l, ea, eb, grp, nxt, nv, nu, tok, hx, by_group(w_gate), by_group(w_up),
                 by_group(w_down), row2(ln_ffn_g[l]), row2(ln_ffn_b[l]))
    return h.reshape(bsz, seq, d)
```

```python
import functools
import math

import jax
import jax.numpy as jnp
from jax import lax
from jax.experimental import pallas as pl
from jax.experimental.pallas import tpu as pltpu

D_MODEL = 1024
D_CONV = 512
D_ATTN = 512
HEAD_DIM = 64
N_HEADS = D_ATTN // HEAD_DIM
CONV_K = 3
N_EXPERTS = 32
N_GROUPS = 8
EXPERTS_PER_GROUP = 4
D_EXPERT = 512
LN_EPS = 1e-5
RMS_EPS = 1e-6

PAIR_A = (0, 0, 0, 1, 1, 2)
PAIR_B = (1, 2, 3, 3, 2, 3)
N_PAIRS = len(PAIR_A)
N_CLASSES = N_GROUPS * N_PAIRS

LANES = 128
SUBLANES = 8
ROW_TILES = D_MODEL // LANES
EXT_TILES = 2 * ROW_TILES

PROJ_ROWS = 1024
ATTN_ROWS = 256
MIX_ROWS = 1024
RANK_ROWS = 512
MOE_ROWS = 128
PLAN_ROWS = 2048

ATTN_STOP = -88.0

VMEM_LIMIT = 56 * 1024 * 1024

_BF16 = jnp.bfloat16
_F32 = jnp.float32


def _dot(a, b):
    return jnp.dot(a, b, preferred_element_type=_F32)


def _dot_nt(a, b):
    return lax.dot_general(a, b, (((1,), (1,)), ((), ())), preferred_element_type=_F32)


def _layer_norm(x, g, b):
    mu = jnp.mean(x, axis=-1, keepdims=True)
    xc = x - mu
    var = jnp.mean(xc * xc, axis=-1, keepdims=True)
    return xc * lax.rsqrt(var + LN_EPS) * g + b


def _rms_norm(x, g):
    ms = jnp.mean(x * x, axis=-1, keepdims=True)
    return x * lax.rsqrt(ms + RMS_EPS) * g


def _load_rows(ref, n_tiles=ROW_TILES):
    rows = ref.shape[0]
    return ref[:, 0:n_tiles, :].reshape(rows, n_tiles * LANES)


def _store_rows(ref, val):
    rows, cols = val.shape
    ref[:, 0:cols // LANES, :] = val.reshape(rows, cols // LANES, LANES)


def _proj_kernel(apply_ln, tiles_per_seq, *refs):
    if apply_ln:
        x_ref, g0_ref, b0_ref, w_ref, cw_ref, gc_ref, h_ref, mc_ref, qkv_ref, carry_ref = refs
    else:
        x_ref, w_ref, cw_ref, gc_ref, mc_ref, qkv_ref, carry_ref = refs
    i = pl.program_id(0)
    rows = x_ref.shape[0]

    if apply_ln:
        h = _layer_norm(x_ref[...], g0_ref[...], b0_ref[...])
        _store_rows(h_ref, h)
    else:
        h = _load_rows(x_ref)
    hb = h.astype(_BF16)

    cb = _dot(hb, w_ref[:, 0:D_CONV])
    cc = _dot(hb, w_ref[:, D_CONV:2 * D_CONV])
    cx = _dot(hb, w_ref[:, 2 * D_CONV:3 * D_CONV])
    u = cc * cx

    @pl.when(i % tiles_per_seq == 0)
    def _():
        carry_ref[...] = jnp.zeros_like(carry_ref)

    prev = carry_ref[...]
    p1 = prev[SUBLANES - 1:SUBLANES, :]
    p2 = prev[SUBLANES - 2:SUBLANES - 1, :]
    row = lax.broadcasted_iota(jnp.int32, (rows, 1), 0)
    u1 = jnp.where(row == 0, p1, pltpu.roll(u, 1, 0))
    u2 = jnp.where(row == 0, p2, jnp.where(row == 1, p1, pltpu.roll(u, 2, 0)))
    carry_ref[...] = u[rows - SUBLANES:rows, :]

    cw = cw_ref[...]
    conv = cb * (cw[0:1, :] * u2 + cw[1:2, :] * u1 + cw[2:3, :] * u)
    mc_ref[...] = _rms_norm(conv, gc_ref[...]).astype(_BF16)

    qkv = _dot(hb, w_ref[:, 3 * D_CONV:])
    qkv_ref[:, 0:D_ATTN] = (qkv[:, 0:D_ATTN] * (1.0 / math.sqrt(HEAD_DIM))).astype(_BF16)
    qkv_ref[:, D_ATTN:] = qkv[:, D_ATTN:].astype(_BF16)


def _proj_conv(x, ln0, w_in_bf, conv_w, g_conv, seq):
    t = x.shape[0]
    rows = PROJ_ROWS
    apply_ln = ln0 is not None
    in_cols = w_in_bf.shape[1]
    row_spec = lambda c: pl.BlockSpec((rows, c), lambda i: (i, 0))
    tile_spec = pl.BlockSpec((rows, ROW_TILES, LANES), lambda i: (i, 0, 0))
    full = lambda a: pl.BlockSpec(a.shape, lambda i: (0,) * a.ndim)
    args, in_specs = [x], [row_spec(D_MODEL) if apply_ln else tile_spec]
    if apply_ln:
        args += [ln0[0], ln0[1]]
        in_specs += [full(ln0[0]), full(ln0[1])]
    args += [w_in_bf, conv_w, g_conv]
    in_specs += [full(w_in_bf), full(conv_w), full(g_conv)]
    out_shape, out_specs = [], []
    if apply_ln:
        out_shape.append(jax.ShapeDtypeStruct((t, ROW_TILES, LANES), _F32))
        out_specs.append(tile_spec)
    out_shape += [jax.ShapeDtypeStruct((t, D_CONV), _BF16),
                  jax.ShapeDtypeStruct((t, in_cols - 3 * D_CONV), _BF16)]
    out_specs += [row_spec(D_CONV), row_spec(in_cols - 3 * D_CONV)]
    return pl.pallas_call(
        functools.partial(_proj_kernel, apply_ln, seq // rows),
        out_shape=out_shape,
        grid=(t // rows,),
        in_specs=in_specs,
        out_specs=out_specs,
        scratch_shapes=[pltpu.VMEM((SUBLANES, D_CONV), _F32)],
        compiler_params=pltpu.CompilerParams(
            dimension_semantics=("arbitrary",), vmem_limit_bytes=VMEM_LIMIT),
        name="proj_conv",
    )(*args)


def _attn_kernel(q_ref, k_ref, v_ref, ux_ref, ga_ref, o_ref, acc_ref, r_ref, lk_ref, ls_ref, sx_ref):
    i = pl.program_id(1)
    tq = q_ref.shape[0]
    tk = tq

    qpos = lax.broadcasted_iota(jnp.int32, (tq, tk), 0)
    kpos = lax.broadcasted_iota(jnp.int32, (tq, tk), 1)
    causal = kpos < qpos

    def block(j, diagonal):
        ks = pl.multiple_of(j * tk, tk)
        for hd in range(N_HEADS):
            cols = slice(hd * HEAD_DIM, (hd + 1) * HEAD_DIM)
            z = _dot_nt(q_ref[:, cols], k_ref[pl.ds(ks, tk), cols])
            zb = z.astype(_BF16)
            nsp = jnp.minimum(-zb, 0.0) - jnp.log(1.0 + jnp.exp(-jnp.abs(zb)))
            ls_ref[hd] = z + nsp.astype(_F32)
            lk_ref[hd] = jnp.where(causal, nsp, jnp.zeros_like(nsp)) if diagonal else nsp
        sx_ref[...] = _dot(lk_ref[...].reshape(N_HEADS * tq, tk), ux_ref[...])
        for hd in range(N_HEADS):
            cols = slice(hd * HEAD_DIM, (hd + 1) * HEAD_DIM)
            rows = slice(hd * tq, (hd + 1) * tq)
            total = jnp.broadcast_to(
                sx_ref[rows, 0:1] + lk_ref[hd, :, 0:1].astype(_F32), (tq, LANES))
            if diagonal:
                a = jnp.where(causal, jnp.exp(ls_ref[hd] + sx_ref[rows, :]), 0.0)
                acc_ref[:, cols] = _dot(a.astype(_BF16), v_ref[pl.ds(ks, tk), cols])
                r_ref[hd] = total
            else:
                r = r_ref[hd]
                arg = ls_ref[hd] + sx_ref[rows, :] + jnp.concatenate([r] * (tk // LANES), axis=1)
                acc_ref[:, cols] += _dot(jnp.exp(arg).astype(_BF16), v_ref[pl.ds(ks, tk), cols])
                r_ref[hd] = r + total

    block(i, True)

    def cond(j):
        return jnp.logical_and(j >= 0, jnp.max(r_ref[...]) > ATTN_STOP)

    def body(j):
        block(j, False)
        return j - 1

    lax.while_loop(cond, body, i - 1)

    o_ref[...] = _rms_norm(acc_ref[...], ga_ref[...]).astype(_BF16)


def _attention(qkv3, ux, g_attn):
    b, s, _ = qkv3.shape
    tq = ATTN_ROWS
    return pl.pallas_call(
        _attn_kernel,
        out_shape=jax.ShapeDtypeStruct((b, s, D_ATTN), _BF16),
        grid=(b, s // tq),
        in_specs=[
            pl.BlockSpec((None, tq, D_ATTN), lambda bi, i: (bi, i, 0)),
            pl.BlockSpec((None, s, D_ATTN), lambda bi, i: (bi, 0, 1)),
            pl.BlockSpec((None, s, D_ATTN), lambda bi, i: (bi, 0, 2)),
            pl.BlockSpec(ux.shape, lambda bi, i: (0, 0)),
            pl.BlockSpec(g_attn.shape, lambda bi, i: (0, 0)),
        ],
        out_specs=pl.BlockSpec((None, tq, D_ATTN), lambda bi, i: (bi, i, 0)),
        scratch_shapes=[pltpu.VMEM((tq, D_ATTN), _F32),
                        pltpu.VMEM((N_HEADS, tq, LANES), _F32),
                        pltpu.VMEM((N_HEADS, tq, tq), _BF16),
                        pltpu.VMEM((N_HEADS, tq, tq), _F32),
                        pltpu.VMEM((N_HEADS * tq, tq), _F32)],
        compiler_params=pltpu.CompilerParams(
            dimension_semantics=("arbitrary", "arbitrary"), vmem_limit_bytes=VMEM_LIMIT),
        name="attention",
    )(qkv3, qkv3, qkv3, ux, g_attn)


def _mix_kernel(alpha, mc_ref, ma_ref, h_ref, wo_ref, g_ref, b_ref, rwh_ref, rwl_ref, rb_ref,
                hx_ref, cls_ref):
    rows = h_ref.shape[0]
    mix = _dot(mc_ref[...], wo_ref[0:D_CONV, :]) + _dot(ma_ref[...], wo_ref[D_CONV:, :])
    h1 = _layer_norm(alpha * _load_rows(h_ref) + mix, g_ref[...], b_ref[...])

    hi = h1.astype(_BF16)
    lo = (h1 - hi.astype(_F32)).astype(_BF16)
    logits = (_dot_nt(rwh_ref[...], hi) + _dot_nt(rwh_ref[...], lo) + _dot_nt(rwl_ref[...], hi))
    score = 1.0 / (1.0 + jnp.exp(-logits))
    sel = score + rb_ref[...]
    s = [sel[j * N_GROUPS:(j + 1) * N_GROUPS, :] for j in range(EXPERTS_PER_GROUP)]
    sc = [score[j * N_GROUPS:(j + 1) * N_GROUPS, :] for j in range(EXPERTS_PER_GROUP)]

    gs = None
    for a in range(EXPERTS_PER_GROUP):
        for c in range(a + 1, EXPERTS_PER_GROUP):
            pair = s[a] + s[c]
            gs = pair if gs is None else jnp.maximum(gs, pair)
    gidx = lax.broadcasted_iota(jnp.int32, gs.shape, 0)
    gmax = jnp.max(gs, axis=0, keepdims=True)
    g = jnp.min(jnp.where(gs == gmax, gidx, N_GROUPS), axis=0, keepdims=True)
    in_g = gidx == g
    sv = [jnp.sum(jnp.where(in_g, s[j], 0.0), axis=0, keepdims=True) for j in range(EXPERTS_PER_GROUP)]
    cv = [jnp.sum(jnp.where(in_g, sc[j], 0.0), axis=0, keepdims=True) for j in range(EXPERTS_PER_GROUP)]

    chosen = []
    for j in range(EXPERTS_PER_GROUP):
        rank = jnp.zeros(sv[j].shape, jnp.int32)
        for c in range(EXPERTS_PER_GROUP):
            if c == j:
                continue
            ahead = (sv[c] >= sv[j]) if c < j else (sv[c] > sv[j])
            rank = rank + ahead.astype(jnp.int32)
        chosen.append(rank < 2)
    den = sum(jnp.where(chosen[j], cv[j], 0.0) for j in range(EXPERTS_PER_GROUP))
    wgt = [jnp.where(chosen[j], cv[j] / den, 0.0) for j in range(EXPERTS_PER_GROUP)]
    first = jnp.full(den.shape, EXPERTS_PER_GROUP, jnp.int32)
    last = jnp.full(den.shape, -1, jnp.int32)
    for j in range(EXPERTS_PER_GROUP):
        first = jnp.where(chosen[j], jnp.minimum(first, j), first)
        last = jnp.where(chosen[j], jnp.maximum(last, j), last)
    pair_id = jnp.zeros(den.shape, jnp.int32)
    for p in range(N_PAIRS):
        pair_id = jnp.where(jnp.logical_and(first == PAIR_A[p], last == PAIR_B[p]), p, pair_id)
    w_first = sum(jnp.where(first == j, wgt[j], 0.0) for j in range(EXPERTS_PER_GROUP))
    w_last = sum(jnp.where(last == j, wgt[j], 0.0) for j in range(EXPERTS_PER_GROUP))
    cls_ref[...] = g * N_PAIRS + pair_id

    sub = lax.broadcasted_iota(jnp.int32, (LANES, rows), 0)
    slab = jnp.where(sub == 0, w_first, jnp.where(sub == 1, w_last, 0.0))
    routed = jnp.concatenate(
        [h1, slab.T, jnp.zeros((rows, (EXT_TILES - ROW_TILES - 1) * LANES), _F32)], axis=1)
    _store_rows(hx_ref, routed)


def _mix_route(alpha, mc, ma, h, w_out_bf, ln_g, ln_b, rw_hi, rw_lo, rb):
    t = h.shape[0]
    rows = MIX_ROWS
    row_spec = lambda c: pl.BlockSpec((rows, c), lambda i: (i, 0))
    full = lambda a: pl.BlockSpec(a.shape, lambda i: (0,) * a.ndim)
    return pl.pallas_call(
        functools.partial(_mix_kernel, alpha),
        out_shape=[jax.ShapeDtypeStruct((t, EXT_TILES, LANES), _F32),
                   jax.ShapeDtypeStruct((1, t), jnp.int32)],
        grid=(t // rows,),
        in_specs=[row_spec(D_CONV), row_spec(D_ATTN),
                  pl.BlockSpec((rows, ROW_TILES, LANES), lambda i: (i, 0, 0)), full(w_out_bf),
                  full(ln_g), full(ln_b), full(rw_hi), full(rw_lo), full(rb)],
        out_specs=[pl.BlockSpec((rows, EXT_TILES, LANES), lambda i: (i, 0, 0)),
                   pl.BlockSpec((1, rows), lambda i: (0, i))],
        compiler_params=pltpu.CompilerParams(
            dimension_semantics=("arbitrary",), vmem_limit_bytes=VMEM_LIMIT),
        name="mix_route",
    )(mc, ma, h, w_out_bf, ln_g, ln_b, rw_hi, rw_lo, rb)


def _rank_kernel(cls_ref, before_ref, ones_ref, rank_ref, cnt_ref, run_ref):
    i = pl.program_id(0)
    rows = cls_ref.shape[1]

    @pl.when(i == 0)
    def _():
        run_ref[...] = jnp.zeros_like(run_ref)

    cidx = lax.broadcasted_iota(jnp.int32, (LANES, rows), 0)
    onehot = cidx == cls_ref[...]
    oh = jnp.where(onehot, 1.0, 0.0).astype(_BF16)
    within = _dot(oh, before_ref[...])
    run = run_ref[...]
    total = within + jnp.concatenate([run] * (rows // LANES), axis=1)
    rank_ref[...] = jnp.sum(jnp.where(onehot, total, 0.0), axis=0, keepdims=True).astype(jnp.int32)
    run = run + _dot(oh, ones_ref[...])
    run_ref[...] = run
    cnt_ref[...] = run


def _rank(cls):
    t = cls.shape[1]
    rows = RANK_ROWS
    idx = jnp.arange(rows)
    before = (idx[:, None] < idx[None, :]).astype(_BF16)
    ones = jnp.ones((rows, LANES), _BF16)
    return pl.pallas_call(
        _rank_kernel,
        out_shape=[jax.ShapeDtypeStruct((1, t), jnp.int32),
                   jax.ShapeDtypeStruct((LANES, LANES), _F32)],
        grid=(t // rows,),
        in_specs=[pl.BlockSpec((1, rows), lambda i: (0, i)),
                  pl.BlockSpec(before.shape, lambda i: (0, 0)),
                  pl.BlockSpec(ones.shape, lambda i: (0, 0))],
        out_specs=[pl.BlockSpec((1, rows), lambda i: (0, i)),
                   pl.BlockSpec((LANES, LANES), lambda i: (0, 0))],
        scratch_shapes=[pltpu.VMEM((LANES, LANES), _F32)],
        compiler_params=pltpu.CompilerParams(dimension_semantics=("arbitrary",)),
        name="rank",
    )(cls, before, ones)


def _plan_kernel(cnt_ref, cls_ref, rank_ref,
                 ea_ref, eb_ref, grp_ref, nxt_ref, nv_ref, nu_ref, tok_ref, start_ref):
    i = pl.program_id(0)
    chunk = cls_ref.shape[0]
    n_slots = tok_ref.shape[0]
    n_blocks = nv_ref.shape[0]

    @pl.when(i == 0)
    def _():
        def pad(s, carry):
            tok_ref[s] = 0
            return carry

        blk = jnp.int32(0)
        for g in range(N_GROUPS):
            for p in range(N_PAIRS):
                c = g * N_PAIRS + p
                n = cnt_ref[c]
                nb = (n + (MOE_ROWS - 1)) // MOE_ROWS
                start = blk * MOE_ROWS
                start_ref[c] = start

                def fill(j, carry, blk=blk, n=n, g=g, p=p):
                    ea_ref[blk + j] = PAIR_A[p]
                    eb_ref[blk + j] = PAIR_B[p]
                    grp_ref[blk + j] = g
                    nv_ref[blk + j] = jnp.minimum(n - j * MOE_ROWS, MOE_ROWS)
                    return carry
                lax.fori_loop(0, nb, fill, 0)
                lax.fori_loop(start + n, start + nb * MOE_ROWS, pad, 0)
                blk = blk + nb
        nu_ref[0] = blk

        def idle(j, carry):
            ea_ref[j] = ea_ref[blk - 1]
            eb_ref[j] = eb_ref[blk - 1]
            grp_ref[j] = grp_ref[blk - 1]
            nxt_ref[j] = -1
            nv_ref[j] = 0
            return carry
        lax.fori_loop(blk, n_blocks, idle, 0)

        nxt_ref[blk - 1] = -1

        def following(k, carry):
            j = blk - 2 - k
            nxt_ref[j] = jnp.where(grp_ref[j + 1] != grp_ref[j], grp_ref[j + 1], nxt_ref[j + 1])
            return carry
        lax.fori_loop(0, blk - 1, following, 0)
        lax.fori_loop(blk * MOE_ROWS, n_slots, pad, 0)

    base = i * chunk

    def place(r, carry):
        tok_ref[start_ref[cls_ref[r]] + rank_ref[r]] = base + r
        return carry
    lax.fori_loop(0, chunk, place, 0, unroll=8)


def _plan(counts, cls, rank, n_blocks):
    t = cls.shape[0]
    chunk = min(PLAN_ROWS, t)
    smem = functools.partial(pl.BlockSpec, memory_space=pltpu.SMEM)
    whole = lambda n: smem((n,), lambda i: (0,))
    i32 = lambda n: jax.ShapeDtypeStruct((n,), jnp.int32)
    return pl.pallas_call(
        _plan_kernel,
        out_shape=[i32(n_blocks), i32(n_blocks), i32(n_blocks), i32(n_blocks), i32(n_blocks), i32(1),
                   i32(n_blocks * MOE_ROWS)],
        grid=(t // chunk,),
        in_specs=[whole(counts.shape[0]), smem((chunk,), lambda i: (i,)), smem((chunk,), lambda i: (i,))],
        out_specs=[whole(n_blocks), whole(n_blocks), whole(n_blocks), whole(n_blocks), whole(n_blocks),
                   whole(1), whole(n_blocks * MOE_ROWS)],
        scratch_shapes=[pltpu.SMEM((N_CLASSES,), jnp.int32)],
        compiler_params=pltpu.CompilerParams(dimension_semantics=("arbitrary",)),
        name="plan",
    )(counts, cls, rank)


def _moe_kernel(alpha, layer, ea_ref, eb_ref, grp_ref, nxt_ref, nv_ref, nu_ref, tok_ref, tokn_ref, tokn2_ref, tokn3_ref, hx_ref,
                wg_hbm, wu_hbm, wd_hbm, g_ref, b_ref, out_ref,
                xbuf, obuf, gsem, ssem, wg_f32, wu_f32, wd_f32, wg_ref, wu_ref, wd_ref, wsem):
    b = pl.program_id(0)
    nu = nu_ref[0]
    rows = MOE_ROWS

    stages = ((wg_hbm, wg_f32, wg_ref), (wu_hbm, wu_f32, wu_ref), (wd_hbm, wd_f32, wd_ref))

    def fetch_group_weights(group):
        for k, (src, stage, _) in enumerate(stages):
            pltpu.make_async_copy(src.at[layer * N_GROUPS + group], stage, wsem.at[k]).start(priority=1)

    def adopt_group_weights():
        for k, (src, stage, dst) in enumerate(stages):
            pltpu.make_async_copy(src.at[0], stage, wsem.at[k]).wait()
            for e in range(EXPERTS_PER_GROUP):
                dst[e] = stage[e].astype(_BF16)

    def gather_start(tokens, s):
        for r in range(rows):
            pltpu.make_async_copy(hx_ref.at[tokens[r]], xbuf.at[s, r], gsem.at[s]).start()

    def gather_wait(s):
        pltpu.make_async_copy(hx_ref.at[pl.ds(0, rows)], xbuf.at[s], gsem.at[s]).wait()

    def scatter_wait(s, n):
        @pl.when(n > 0)
        def _():
            pltpu.make_async_copy(obuf.at[s, pl.ds(0, n)], out_ref.at[pl.ds(0, n)], ssem.at[s]).wait()

    def step(s, gs):
        o = 1 - s
        gn = (gs + 3) % 4

        @pl.when(b == 0)
        def _():
            gather_start(tok_ref, gs)
            gather_start(tokn_ref, (gs + 1) % 4)
            gather_start(tokn2_ref, (gs + 2) % 4)
            fetch_group_weights(grp_ref[b])

        @pl.when(jnp.logical_or(b == 0, grp_ref[b] != grp_ref[jnp.maximum(b - 1, 0)]))
        def _():
            adopt_group_weights()

            @pl.when(nxt_ref[b] >= 0)
            def _():
                fetch_group_weights(nxt_ref[b])
        gather_wait(gs)
        gather_start(tokn3_ref, gn)

        @pl.when(b >= 2)
        def _():
            scatter_wait(s, nv_ref[jnp.maximum(b - 2, 0)])

        xe = _load_rows(xbuf.at[gs], ROW_TILES + SUBLANES)
        x = xe[:, 0:D_MODEL]
        xb = x.astype(_BF16)
        wts = xe[:, D_MODEL:D_MODEL + LANES]
        y = None
        for lane, e_ref in enumerate((ea_ref, eb_ref)):
            e = e_ref[b]
            a = _dot(xb, wg_ref[e])
            u = _dot(xb, wu_ref[e])
            hid = a / (1.0 + jnp.exp(-a)) * u * wts[:, lane:lane + 1]
            part = _dot(hid.astype(_BF16), wd_ref[e])
            y = part if y is None else y + part
        _store_rows(obuf.at[s], _layer_norm(alpha * x + y, g_ref[...], b_ref[...]))

        n_valid = nv_ref[b]

        @pl.when(n_valid == rows)
        def _():
            for r in range(rows):
                pltpu.make_async_copy(obuf.at[s, r], out_ref.at[tok_ref[r]], ssem.at[s]).start()

        @pl.when(n_valid < rows)
        def _():
            for r in range(rows):
                @pl.when(r < n_valid)
                def _():
                    pltpu.make_async_copy(obuf.at[s, r], out_ref.at[tok_ref[r]], ssem.at[s]).start()

        @pl.when(b == nu - 1)
        def _():
            gather_wait((gs + 1) % 4)
            gather_wait((gs + 2) % 4)
            gather_wait(gn)

            @pl.when(b >= 1)
            def _():
                scatter_wait(o, nv_ref[jnp.maximum(b - 1, 0)])
            scatter_wait(s, n_valid)

    for k in range(4):
        pl.when(jnp.logical_and(b < nu, b % 4 == k))(functools.partial(step, k % 2, k % 4))


def _moe(alpha, layer, ea, eb, grp, nxt, nv, nu, tok, hx, wg, wu, wd, ln_g, ln_b):
    t = hx.shape[0]
    n_blocks = nv.shape[0]
    rows = MOE_ROWS
    last = n_blocks - 1
    full = lambda a: pl.BlockSpec(a.shape, lambda i, ea, eb, grp, nxt, nv, nu: (0,) * a.ndim)
    grid_spec = pltpu.PrefetchScalarGridSpec(
        num_scalar_prefetch=6,
        grid=(n_blocks,),
        in_specs=[pl.BlockSpec((rows,), lambda i, ea, eb, grp, nxt, nv, nu: (i,), memory_space=pltpu.SMEM),
                  pl.BlockSpec((rows,), lambda i, ea, eb, grp, nxt, nv, nu: (jnp.minimum(i + 1, last),),
                               memory_space=pltpu.SMEM),
                  pl.BlockSpec((rows,), lambda i, ea, eb, grp, nxt, nv, nu: (jnp.minimum(i + 2, last),),
                               memory_space=pltpu.SMEM),
                  pl.BlockSpec((rows,), lambda i, ea, eb, grp, nxt, nv, nu: (jnp.minimum(i + 3, last),),
                               memory_space=pltpu.SMEM),
                  pl.BlockSpec(memory_space=pl.ANY),
                  pl.BlockSpec(memory_space=pl.ANY), pl.BlockSpec(memory_space=pl.ANY),
                  pl.BlockSpec(memory_space=pl.ANY), full(ln_g), full(ln_b)],
        out_specs=pl.BlockSpec(memory_space=pl.ANY),
        scratch_shapes=[pltpu.VMEM((4, rows, EXT_TILES, LANES), _F32),
                        pltpu.VMEM((2, rows, ROW_TILES, LANES), _F32),
                        pltpu.SemaphoreType.DMA((4,)),
                        pltpu.SemaphoreType.DMA((2,)),
                        pltpu.VMEM(wg.shape[1:], _F32), pltpu.VMEM(wu.shape[1:], _F32),
                        pltpu.VMEM(wd.shape[1:], _F32),
                        pltpu.VMEM(wg.shape[1:], _BF16), pltpu.VMEM(wu.shape[1:], _BF16),
                        pltpu.VMEM(wd.shape[1:], _BF16),
                        pltpu.SemaphoreType.DMA((3,))],
    )
    return pl.pallas_call(
        functools.partial(_moe_kernel, alpha, layer),
        out_shape=jax.ShapeDtypeStruct((t, ROW_TILES, LANES), _F32),
        grid_spec=grid_spec,
        compiler_params=pltpu.CompilerParams(
            dimension_semantics=("arbitrary",), vmem_limit_bytes=VMEM_LIMIT),
        name="moe",
    )(ea, eb, grp, nxt, nv, nu, tok, tok, tok, tok, hx, wg, wu, wd, ln_g, ln_b)


def kernel(x, ln0_g, ln0_b, w_in, conv_w, g_conv, g_attn, w_out, ln_mix_g, ln_mix_b,
           router_w, router_b, w_gate, w_up, w_down, ln_ffn_g, ln_ffn_b):
    bsz, seq, d = x.shape
    depth = w_in.shape[0]
    t = bsz * seq
    alpha = (2.0 * depth) ** 0.25
    row2 = lambda v: v.reshape(1, -1)

    order = jnp.asarray([(r % N_GROUPS) * EXPERTS_PER_GROUP + r // N_GROUPS for r in range(N_EXPERTS)])
    rw = router_w.T[order]
    rw_hi = rw.astype(_BF16)
    rw_lo = (rw - rw_hi.astype(_F32)).astype(_BF16)
    rb = router_b[order].reshape(N_EXPERTS, 1)

    kidx = jnp.arange(ATTN_ROWS)
    ux = (kidx[:, None] > kidx[None, :]).astype(_BF16)

    by_group = lambda w: w.reshape((depth * N_GROUPS, EXPERTS_PER_GROUP) + w.shape[2:])

    n_slots = (t + N_CLASSES * (MOE_ROWS - 1) + MOE_ROWS - 1) // MOE_ROWS * MOE_ROWS
    n_blocks = n_slots // MOE_ROWS

    h = x.reshape(t, d)
    for l in range(depth):
        w_in_bf = w_in[l].astype(_BF16)
        if l == 0:
            h, mc, qkv = _proj_conv(h, (row2(ln0_g), row2(ln0_b)), w_in_bf, conv_w[l],
                                    row2(g_conv[l]), seq)
        else:
            mc, qkv = _proj_conv(h, None, w_in_bf, conv_w[l], row2(g_conv[l]), seq)
        ma = _attention(qkv.reshape(bsz, seq, 3 * D_ATTN), ux, row2(g_attn[l])).reshape(t, D_ATTN)
        hx, cls = _mix_route(alpha, mc, ma, h, w_out[l].astype(_BF16), row2(ln_mix_g[l]),
                             row2(ln_mix_b[l]), rw_hi, rw_lo, rb)
        rank, cnt = _rank(cls)
        counts = cnt[:, 0].astype(jnp.int32)
        ea, eb, grp, nxt, nv, nu, tok = _plan(counts, cls.reshape(t), rank.reshape(t), n_blocks)
        h = _moe(alpha, l, ea, eb, grp, nxt, nv, nu, tok, hx, by_group(w_gate), by_group(w_up),
                 by_group(w_down), row2(ln_ffn_g[l]), row2(ln_ffn_b[l]))
    return h.reshape(bsz, seq, d)
```
